```python
import math
import jax, jax.numpy as jnp
from jax import lax
import numpy as np

D_MODEL = 4096
BATCH = 4
SEQ = 2048
DEPTH = 1
DEC_BATCH = 128
DEC_SEQ = 1
PAST_LEN = 16384
PAGE_SIZE = 128

HG_HEADS = 32
HG_DK = 128
HG_DV = D_MODEL // HG_HEADS
HG_WIDTH = HG_HEADS * HG_DK
HG_VWIDTH = HG_HEADS * HG_DV
HG_CHUNK = 64
SSM_INNER = 2 * D_MODEL
SSM_HEADDIM = 64
SSM_HEADS = SSM_INNER // SSM_HEADDIM
SSM_GROUPS = 8
SSM_HPG = SSM_HEADS // SSM_GROUPS
SSM_STATE = 128
SSM_CONV = 4
SSM_CONV_DIM = SSM_INNER + 2 * SSM_GROUPS * SSM_STATE
SSM_CHUNK = 128
IN_SPLITS = (HG_WIDTH, HG_WIDTH, HG_VWIDTH, HG_VWIDTH, SSM_INNER, SSM_CONV_DIM, SSM_HEADS, D_MODEL, D_MODEL)
D_IN_PROJ = 2 * HG_WIDTH + 2 * HG_VWIDTH + SSM_INNER + SSM_CONV_DIM + SSM_HEADS + 2 * D_MODEL
N_GROUPS = 4
EXPERTS_PER_GROUP = 4
N_EXPERTS = N_GROUPS * EXPERTS_PER_GROUP
TOP_K = 2
D_EXPERT = 1024
N_MOD = 6
EPS = 1e-6

kernel_name = 'hgrn2_mamba2_hmoe_adaln_step'


def rms_norm(x, g):
    xf = x.astype(jnp.float32)
    y = xf * lax.rsqrt(jnp.mean(xf * xf, axis=-1, keepdims=True) + EPS)
    return (y * g.astype(jnp.float32)).astype(x.dtype)


def split_cols(u, sizes):
    return jnp.split(u, np.cumsum(sizes)[:-1].tolist(), axis=-1)


def hgrn_chunked(q, k, v, logf, s0):
    bsz, t, nh, _ = q.shape
    nc = t // HG_CHUNK

    def to_chunks(a):
        return a.reshape(bsz, nc, HG_CHUNK, nh, a.shape[-1]).transpose(1, 0, 3, 2, 4)

    causal = jnp.tril(jnp.ones((HG_CHUNK, HG_CHUNK), dtype=bool))[:, :, None]

    def step(s, inp):
        qb, kb, vb, gb = inp
        b = jnp.cumsum(gb, axis=2)
        o_inter = jnp.einsum('bhtk,bhkv->bhtv', qb * jnp.exp(b), s)
        rel = b[:, :, :, None, :] - b[:, :, None, :, :]
        decay = jnp.exp(jnp.where(causal, rel, -jnp.inf))
        scores = jnp.einsum('bhtk,bhtsk,bhsk->bhts', qb, decay, kb)
        o = o_inter + jnp.einsum('bhts,bhsv->bhtv', scores, vb)
        b_end = b[:, :, -1:, :]
        s_new = jnp.exp(b_end[:, :, 0, :])[..., None] * s + jnp.einsum('bhsk,bhsv->bhkv', kb * jnp.exp(b_end - b), vb)
        return s_new, o

    s_fin, o = lax.scan(step, s0, (to_chunks(q), to_chunks(k), to_chunks(v), to_chunks(logf)))
    o = o.transpose(1, 0, 3, 2, 4).reshape(bsz, t, nh, v.shape[-1])
    return o, s_fin


def hgrn_recurrent(q, k, v, logf, s0):
    def step(s, inp):
        qt, kt, vt, gt = inp
        s = jnp.exp(gt)[..., None] * s + kt[..., :, None] * vt[..., None, :]
        return s, jnp.einsum('bhk,bhkv->bhv', qt, s)

    s_fin, o = lax.scan(step, s0, tuple(jnp.swapaxes(a, 0, 1) for a in (q, k, v, logf)))
    return jnp.swapaxes(o, 0, 1), s_fin


def ssd_chunked(xdt, a, bm, cm, h0):
    bsz, t = xdt.shape[:2]
    nc = t // SSM_CHUNK
    xc = xdt.reshape(bsz, nc, SSM_CHUNK, SSM_GROUPS, SSM_HPG, SSM_HEADDIM).transpose(1, 0, 2, 3, 4, 5)
    ac = a.reshape(bsz, nc, SSM_CHUNK, SSM_GROUPS, SSM_HPG).transpose(1, 0, 3, 4, 2)
    bc = bm.reshape(bsz, nc, SSM_CHUNK, SSM_GROUPS, SSM_STATE).transpose(1, 0, 2, 3, 4)
    cc = cm.reshape(bsz, nc, SSM_CHUNK, SSM_GROUPS, SSM_STATE).transpose(1, 0, 2, 3, 4)
    causal = jnp.tril(jnp.ones((SSM_CHUNK, SSM_CHUNK), dtype=bool))

    def step(h, inp):
        xb, ab, bb, cb_ = inp
        cum = jnp.cumsum(ab, axis=-1)
        seg = cum[..., :, None] - cum[..., None, :]
        decay = jnp.exp(jnp.where(causal, seg, -jnp.inf))
        cbm = jnp.einsum('btgn,bsgn->bgts', cb_, bb)
        y_intra = jnp.einsum('bgts,bgets,bsgep->btgep', cbm, decay, xb)
        y_inter = jnp.einsum('btgn,bgepn,bget->btgep', cb_, h, jnp.exp(cum))
        w_end = jnp.exp(cum[..., -1:] - cum)
        h_new = jnp.exp(cum[..., -1])[..., None, None] * h + jnp.einsum('bges,bsgep,bsgn->bgepn', w_end, xb, bb)
        return h_new, y_intra + y_inter

    h_fin, y = lax.scan(step, h0, (xc, ac, bc, cc))
    y = y.transpose(1, 0, 2, 3, 4, 5).reshape(bsz, t, SSM_GROUPS, SSM_HPG, SSM_HEADDIM)
    return y, h_fin


def ssd_recurrent(xdt, a, bm, cm, h0):
    def step(h, inp):
        xt, at, bt, ct = inp
        h = jnp.exp(at)[..., None, None] * h + jnp.einsum('bgep,bgn->bgepn', xt, bt)
        return h, jnp.einsum('bgepn,bgn->bgep', h, ct)

    h_fin, y = lax.scan(step, h0, tuple(jnp.swapaxes(v, 0, 1) for v in (xdt, a, bm, cm)))
    return jnp.swapaxes(y, 0, 1), h_fin


def causal_conv_silu(xpad, w, b):
    t = xpad.shape[1] - SSM_CONV + 1
    out = sum(xpad[:, j:j + t] * w[j] for j in range(SSM_CONV)) + b
    return jax.nn.silu(out)


def hier_moe(h, w_rg, b_rg, w_re, b_re, w_gate_e, w_up_e, w_down_e):
    n = h.shape[0]
    g_logits = (h @ w_rg).astype(jnp.float32) + b_rg.astype(jnp.float32)
    g_prob = jax.nn.softmax(g_logits, axis=-1)
    _, g_idx = lax.top_k(g_logits, 1)
    g_onehot = jax.nn.one_hot(g_idx[:, 0], N_GROUPS, dtype=jnp.float32)
    p_group = jnp.sum(g_prob * g_onehot, axis=-1, keepdims=True)
    e_logits = ((h @ w_re).astype(jnp.float32) + b_re.astype(jnp.float32)).reshape(n, N_GROUPS, EXPERTS_PER_GROUP)
    e_in = jnp.einsum('ng,nge->ne', g_onehot, e_logits)
    e_top, e_idx = lax.top_k(e_in, TOP_K)
    wts = jax.nn.softmax(e_top, axis=-1) * p_group
    expert_id = g_idx * EXPERTS_PER_GROUP + e_idx
    combine = jnp.einsum('nk,nke->ne', wts, jax.nn.one_hot(expert_id, N_EXPERTS, dtype=jnp.float32))
    out = jnp.zeros((n, D_MODEL), jnp.float32)
    for e in range(N_EXPERTS):
        act = jax.nn.silu(h @ w_gate_e[e]) * (h @ w_up_e[e])
        out = out + combine[:, e:e + 1] * (act @ w_down_e[e]).astype(jnp.float32)
    return out.astype(h.dtype)


def decoder_layer(x, c, s_hg, s_conv, s_ssm, lb, w_mod, b_mod, norm1_g, norm2_g, w_in, hg_norm_g, conv_w, conv_b,
                  dt_bias, a_log, d_skip, ssm_norm_g, w_branch_a, w_branch_b, w_out, w_rg, b_rg, w_re, b_re,
                  w_gate_e, w_up_e, w_down_e, chunked):
    f32 = jnp.float32
    bsz, t, _ = x.shape
    mod = jax.nn.silu(c) @ w_mod + b_mod
    sh1, sc1, g1, sh2, sc2, g2 = [m[:, None, :] for m in jnp.split(mod, N_MOD, axis=-1)]

    h = rms_norm(x, norm1_g) * (1 + sc1) + sh1
    u = h @ w_in
    q_raw, f_raw, i_raw, og_raw, z, xbc, dt_raw, ga_raw, gb_raw = split_cols(u, IN_SPLITS)

    fgate = lb + (1.0 - lb) * jax.nn.sigmoid(f_raw.astype(f32))
    logf = jnp.log(fgate)
    kk = 1.0 - fgate
    qq = jax.nn.silu(q_raw.astype(f32))
    hd = lambda a, d: a.reshape(bsz, t, HG_HEADS, d)
    hg_args = (hd(qq, HG_DK), hd(kk, HG_DK), hd(i_raw.astype(f32), HG_DV), hd(logf, HG_DK), s_hg.astype(f32))
    o_a, s_hg_new = hgrn_chunked(*hg_args) if chunked else hgrn_recurrent(*hg_args)
    o_a = rms_norm(o_a, hg_norm_g.reshape(HG_HEADS, HG_DV)).reshape(bsz, t, HG_VWIDTH)
    o_a = o_a * jax.nn.silu(og_raw.astype(f32))

    xpad = jnp.concatenate([s_conv.astype(xbc.dtype), xbc], axis=1)
    s_conv_new = xpad[:, -(SSM_CONV - 1):]
    xbc_c = causal_conv_silu(xpad, conv_w, conv_b)
    xs, bm, cm = split_cols(xbc_c, (SSM_INNER, SSM_GROUPS * SSM_STATE, SSM_GROUPS * SSM_STATE))
    dt = jax.nn.softplus(dt_raw.astype(f32) + dt_bias.astype(f32))
    a = dt * (-jnp.exp(a_log.astype(f32)))
    xh = xs.astype(f32).reshape(bsz, t, SSM_GROUPS, SSM_HPG, SSM_HEADDIM)
    ssd_args = (xh * dt.reshape(bsz, t, SSM_GROUPS, SSM_HPG)[..., None],
                a.reshape(bsz, t, SSM_GROUPS, SSM_HPG),
                bm.astype(f32).reshape(bsz, t, SSM_GROUPS, SSM_STATE),
                cm.astype(f32).reshape(bsz, t, SSM_GROUPS, SSM_STATE),
                s_ssm.astype(f32).reshape(bsz, SSM_GROUPS, SSM_HPG, SSM_HEADDIM, SSM_STATE))
    y_b, s_ssm_new = ssd_chunked(*ssd_args) if chunked else ssd_recurrent(*ssd_args)
    y_b = y_b + d_skip.astype(f32).reshape(SSM_GROUPS, SSM_HPG)[..., None] * xh
    y_b = y_b.reshape(bsz, t, SSM_INNER) * jax.nn.silu(z.astype(f32))
    y_b = rms_norm(y_b.reshape(bsz, t, SSM_GROUPS, SSM_INNER // SSM_GROUPS),
                   ssm_norm_g.reshape(SSM_GROUPS, SSM_INNER // SSM_GROUPS)).reshape(bsz, t, SSM_INNER)

    merged = (jax.nn.sigmoid(ga_raw) * (o_a.astype(x.dtype) @ w_branch_a)
              + jax.nn.sigmoid(gb_raw) * (y_b.astype(x.dtype) @ w_branch_b))
    x = x + g1 * (merged @ w_out)

    h2 = rms_norm(x, norm2_g) * (1 + sc2) + sh2
    moe = hier_moe(h2.reshape(bsz * t, D_MODEL), w_rg, b_rg, w_re, b_re, w_gate_e, w_up_e, w_down_e)
    x = x + g2 * moe.reshape(bsz, t, D_MODEL)

    s_ssm_new = s_ssm_new.reshape(bsz, SSM_HEADS, SSM_HEADDIM, SSM_STATE).astype(s_ssm.dtype)
    return x, s_hg_new.astype(s_hg.dtype), s_conv_new.astype(s_conv.dtype), s_ssm_new


def setup_inputs(seed: int = 0) -> dict:
    key = jax.random.key(seed)
    ks = iter(jax.random.split(key, 40))
    nrm = lambda shape, s: jax.random.normal(next(ks), shape, jnp.float32) * s
    gain = lambda shape: 1.0 + 0.05 * jax.random.normal(next(ks), shape, jnp.float32)
    dt0 = jnp.exp(jax.random.uniform(next(ks), (DEPTH, SSM_HEADS), jnp.float32, math.log(1e-3), math.log(1e-1)))
    return {
        'x_prompt': nrm((BATCH, SEQ, D_MODEL), 1.0),
        'x_sample': nrm((DEC_BATCH, DEC_SEQ, D_MODEL), 1.0),
        'state_hgrn': nrm((DEPTH, DEC_BATCH, HG_HEADS, HG_DK, HG_DV), 0.5),
        'state_conv': nrm((DEPTH, DEC_BATCH, SSM_CONV - 1, SSM_CONV_DIM), 1.0),
        'state_ssm': nrm((DEPTH, DEC_BATCH, SSM_HEADS, SSM_HEADDIM, SSM_STATE), 0.3),
        'c_prompt': nrm((BATCH, D_MODEL), 1.0),
        'c_sample': nrm((DEC_BATCH, D_MODEL), 1.0),
        'lb_params': nrm((DEPTH + 1, HG_WIDTH), 1.0),
        'w_mod': nrm((DEPTH, D_MODEL, N_MOD * D_MODEL), 0.5 * D_MODEL ** -0.5),
        'b_mod': nrm((DEPTH, N_MOD * D_MODEL), 0.02),
        'norm1_g': gain((DEPTH, D_MODEL)),
        'norm2_g': gain((DEPTH, D_MODEL)),
        'w_in': nrm((DEPTH, D_MODEL, D_IN_PROJ), D_MODEL ** -0.5),
        'hg_norm_g': gain((DEPTH, HG_VWIDTH)),
        'conv_w': nrm((DEPTH, SSM_CONV, SSM_CONV_DIM), SSM_CONV ** -0.5),
        'conv_b': nrm((DEPTH, SSM_CONV_DIM), 0.02),
        'dt_bias': dt0 + jnp.log(-jnp.expm1(-dt0)),
        'a_log': jnp.log(jax.random.uniform(next(ks), (DEPTH, SSM_HEADS), jnp.float32, 1.0, 16.0)),
        'd_skip': gain((DEPTH, SSM_HEADS)),
        'ssm_norm_g': gain((DEPTH, SSM_INNER)),
        'w_branch_a': nrm((DEPTH, HG_VWIDTH, D_MODEL), HG_VWIDTH ** -0.5),
        'w_branch_b': nrm((DEPTH, SSM_INNER, D_MODEL), SSM_INNER ** -0.5),
        'w_out': nrm((DEPTH, D_MODEL, D_MODEL), D_MODEL ** -0.5),
        'w_router_group': nrm((DEPTH, D_MODEL, N_GROUPS), D_MODEL ** -0.5),
        'b_router_group': nrm((DEPTH, N_GROUPS), 0.01),
        'w_router_expert': nrm((DEPTH, D_MODEL, N_EXPERTS), D_MODEL ** -0.5),
        'b_router_expert': nrm((DEPTH, N_EXPERTS), 0.01),
        'w_gate_e': nrm((DEPTH, N_EXPERTS, D_MODEL, D_EXPERT), D_MODEL ** -0.5),
        'w_up_e': nrm((DEPTH, N_EXPERTS, D_MODEL, D_EXPERT), D_MODEL ** -0.5),
        'w_down_e': nrm((DEPTH, N_EXPERTS, D_EXPERT, D_MODEL), D_EXPERT ** -0.5),
        'final_g': gain((D_MODEL,)),
    }


def reference(x_prompt, x_sample, state_hgrn, state_conv, state_ssm, c_prompt, c_sample, lb_params, w_mod, b_mod,
              norm1_g, norm2_g, w_in, hg_norm_g, conv_w, conv_b, dt_bias, a_log, d_skip, ssm_norm_g, w_branch_a,
              w_branch_b, w_out, w_router_group, b_router_group, w_router_expert, b_router_expert, w_gate_e, w_up_e,
              w_down_e, final_g):
    lb_all = jnp.cumsum(jax.nn.softmax(lb_params.astype(jnp.float32), axis=0), axis=0)
    n_prompt = x_prompt.shape[0]
    yp, ys = x_prompt, x_sample
    hgp, cvp, ssp, hgs, cvs, sss = [], [], [], [], [], []
    for l in range(DEPTH):
        lw = (w_mod[l], b_mod[l], norm1_g[l], norm2_g[l], w_in[l], hg_norm_g[l], conv_w[l], conv_b[l], dt_bias[l],
              a_log[l], d_skip[l], ssm_norm_g[l], w_branch_a[l], w_branch_b[l], w_out[l], w_router_group[l],
              b_router_group[l], w_router_expert[l], b_router_expert[l], w_gate_e[l], w_up_e[l], w_down_e[l])
        z_hg = jnp.zeros((n_prompt,) + state_hgrn.shape[2:], state_hgrn.dtype)
        z_cv = jnp.zeros((n_prompt,) + state_conv.shape[2:], state_conv.dtype)
        z_ss = jnp.zeros((n_prompt,) + state_ssm.shape[2:], state_ssm.dtype)
        yp, a1, a2, a3 = decoder_layer(yp, c_prompt, z_hg, z_cv, z_ss, lb_all[l], *lw, chunked=True)
        ys, b1, b2, b3 = decoder_layer(ys, c_sample, state_hgrn[l], state_conv[l], state_ssm[l], lb_all[l], *lw,
                                       chunked=False)
        hgp.append(a1); cvp.append(a2); ssp.append(a3)
        hgs.append(b1); cvs.append(b2); sss.append(b3)
    y_prompt = rms_norm(yp, final_g)
    y_sample = rms_norm(ys, final_g)
    hgrn_prompt = jnp.stack(hgp)
    conv_prompt = jnp.stack(cvp)
    ssm_prompt = jnp.stack(ssp)
    hgrn_sample = jnp.stack(hgs)
    conv_sample = jnp.stack(cvs)
    ssm_sample = jnp.stack(sss)
    return (y_prompt, y_sample, hgrn_prompt, conv_prompt, ssm_prompt, hgrn_sample, conv_sample, ssm_sample)
```

```python
import functools

import numpy as np
import jax
import jax.numpy as jnp
from jax import lax
from jax.experimental import pallas as pl
from jax.experimental.pallas import tpu as pltpu

F32 = jnp.float32
BF16 = jnp.bfloat16
EPS = 1e-6

VMEM_LIMIT_BYTES = 56 * 1024 * 1024
LANES = 128

HG_DK = 128
HG_CHUNK = 128
SSM_HEADDIM = 64
SSM_STATE = 128
SSM_HPG = 16
SSM_CHUNK = 128
SSM_CONV = 4
N_GROUPS = 4
EXPERTS_PER_GROUP = 4
N_EXPERTS = 16
MOE_TILE = 256
MOE_NCHUNK = 2


def _params(sem):
    return pltpu.CompilerParams(dimension_semantics=sem, vmem_limit_bytes=VMEM_LIMIT_BYTES)


def _sigmoid(x):
    return 1.0 / (1.0 + jnp.exp(-x))


def _silu(x):
    return x * _sigmoid(x)


def _softplus(x):
    return jnp.maximum(x, 0.0) + jnp.log(1.0 + jnp.exp(-jnp.abs(x)))


def _split3(x):
    p1 = x.astype(BF16)
    r1 = x - p1.astype(F32)
    p2 = r1.astype(BF16)
    r2 = r1 - p2.astype(F32)
    return p1, p2, r2.astype(BF16)


def _dot(a, b):
    return jnp.dot(a, b, preferred_element_type=F32)


def _dot_nt(a, b):
    return lax.dot_general(a, b, (((1,), (1,)), ((), ())), preferred_element_type=F32)


def _dot_tn(a, b):
    return lax.dot_general(a, b, (((0,), (0,)), ((), ())), preferred_element_type=F32)


def _exact_dot(w_bf16, x_f32):
    n = x_f32.shape[-1]
    p1, p2, p3 = _split3(x_f32)
    r = _dot(w_bf16, jnp.concatenate([p1, p2, p3], axis=-1))
    return r[:, :n] + r[:, n:2 * n] + r[:, 2 * n:]


def _exact_dot_rhs(x_f32, w_bf16):
    p1, p2, p3 = _split3(x_f32)
    return _dot(p1, w_bf16) + _dot(p2, w_bf16) + _dot(p3, w_bf16)


def _pcall(body, *, grid, in_specs, args, outs, sem, name, scratch=(), num_prefetch=0):
    n_in = len(args) - num_prefetch
    into = [(i, o[3]) for i, o in enumerate(outs) if o[3] is not None]
    in_specs = list(in_specs) + [pl.BlockSpec(memory_space=pl.ANY)] * len(into)
    all_args = list(args) + [buf for _, buf in into]
    aliases = {len(args) + j: i for j, (i, _) in enumerate(into)}

    def wrapped(*refs):
        keep = refs[:num_prefetch + n_in] + refs[num_prefetch + n_in + len(into):]
        return body(*keep)

    grid_spec = pltpu.PrefetchScalarGridSpec(
        num_scalar_prefetch=num_prefetch, grid=grid, in_specs=in_specs,
        out_specs=[o[2] for o in outs], scratch_shapes=list(scratch))
    res = pl.pallas_call(
        wrapped, grid_spec=grid_spec,
        out_shape=[jax.ShapeDtypeStruct(o[0], o[1]) for o in outs],
        input_output_aliases=aliases, compiler_params=_params(sem), name=name,
    )(*all_args)
    return res


def _mm_body(a_ref, b_ref, *rest, n_extra, a_fn, epilogue):
    extras = rest[:n_extra]
    o_ref = rest[n_extra]
    a = a_ref[...]
    if a_fn is not None:
        a = a_fn(a)
    acc = _dot(a.astype(BF16), b_ref[...].astype(BF16))
    if epilogue is not None:
        acc = epilogue(acc, *[e[...] for e in extras])
    o_ref[...] = acc.astype(o_ref.dtype)


def _mm(a, b, *, tm, tn, name, k=None, a_kblk=0, b_kblk=0, col0=0, ncols=None, extras=(), a_fn=None,
        epilogue=None, out_dtype=F32):
    m = a.shape[0]
    k = a.shape[1] if k is None else k
    ncols = b.shape[1] if ncols is None else ncols
    assert m % tm == 0 and ncols % tn == 0 and col0 % tn == 0
    cb = col0 // tn
    in_specs = [pl.BlockSpec((tm, k), lambda i, j: (i, a_kblk)),
                pl.BlockSpec((k, tn), lambda i, j: (b_kblk, j + cb))]
    in_specs += [s for _, s in extras]
    return pl.pallas_call(
        functools.partial(_mm_body, n_extra=len(extras), a_fn=a_fn, epilogue=epilogue),
        grid=(m // tm, ncols // tn),
        in_specs=in_specs,
        out_specs=pl.BlockSpec((tm, tn), lambda i, j: (i, j)),
        out_shape=jax.ShapeDtypeStruct((m, ncols), out_dtype),
        compiler_params=_params(("parallel", "arbitrary")),
        name=name,
    )(a, b, *[x for x, _ in extras])


def _rms(x, g):
    return x * lax.rsqrt(jnp.mean(x * x, axis=-1, keepdims=True) + EPS) * g


def _mod_view(mod, per_row):
    return mod if per_row else mod.reshape(mod.shape[0], 1, mod.shape[1])


def _mod_spec(per_row, tr, d, col_blk, nt, row0):
    if per_row:
        rb = row0 // tr
        return pl.BlockSpec((tr, d), lambda b, t: (rb + t, col_blk))
    return pl.BlockSpec((1, 1, d), lambda b, t: (row0 + b, 0, col_blk))


def _rows_spec(tr, d, nt, row0):
    rb = row0 // tr
    return pl.BlockSpec((tr, d), lambda b, t: (rb + b * nt + t, 0))


def _norm_mod_body(x_ref, sc_ref, sh_ref, g_ref, o_ref):
    d = x_ref.shape[-1]
    sc = sc_ref[...].reshape(-1, d)
    sh = sh_ref[...].reshape(-1, d)
    o_ref[...] = (_rms(x_ref[...], g_ref[...]) * (1.0 + sc) + sh).astype(o_ref.dtype)


def _norm_mod(x, mod, g, *, nb, nt, tr, per_row, mod_row0, sc_blk, sh_blk, into, total_rows, out_row0, name):
    d = x.shape[-1]
    mv = _mod_view(mod, per_row)
    return _pcall(
        _norm_mod_body, grid=(nb, nt),
        in_specs=[_rows_spec(tr, d, nt, 0),
                  _mod_spec(per_row, tr, d, sc_blk, nt, mod_row0),
                  _mod_spec(per_row, tr, d, sh_blk, nt, mod_row0),
                  pl.BlockSpec((1, d), lambda b, t: (0, 0))],
        args=[x, mv, mv, g],
        outs=[((total_rows, d), BF16, _rows_spec(tr, d, nt, out_row0), into)],
        sem=("parallel", "arbitrary"), name=name)[0]


def _router_math(lg):
    lane = lax.broadcasted_iota(jnp.int32, lg.shape, 1)
    neg = jnp.float32(-jnp.inf)
    gm = jnp.where(lane < N_GROUPS, lg, neg)
    gmax = jnp.max(gm, axis=-1, keepdims=True)
    gidx = jnp.min(jnp.where(gm == gmax, lane, LANES), axis=-1, keepdims=True)
    gsum = jnp.sum(jnp.where(lane < N_GROUPS, jnp.exp(gm - gmax), 0.0), axis=-1, keepdims=True)
    p_group = 1.0 / gsum
    lo = N_GROUPS + EXPERTS_PER_GROUP * gidx
    em = jnp.where((lane >= lo) & (lane < lo + EXPERTS_PER_GROUP), lg, neg)
    m1 = jnp.max(em, axis=-1, keepdims=True)
    i1 = jnp.min(jnp.where(em == m1, lane, LANES), axis=-1, keepdims=True)
    em2 = jnp.where(lane == i1, neg, em)
    m2 = jnp.max(em2, axis=-1, keepdims=True)
    i2 = jnp.min(jnp.where(em2 == m2, lane, LANES), axis=-1, keepdims=True)
    r = jnp.exp(m2 - m1)
    w1 = p_group / (1.0 + r)
    w2 = p_group * r / (1.0 + r)
    e1 = (i1 - N_GROUPS).astype(F32)
    e2 = (i2 - N_GROUPS).astype(F32)
    out = jnp.where(lane == 0, w1, 0.0)
    out = jnp.where(lane == 1, w2, out)
    out = jnp.where(lane == 2, e1, out)
    out = jnp.where(lane == 3, e2, out)
    return out


def _resid_norm_router_body(x_ref, p_ref, gt_ref, sc_ref, sh_ref, g_ref, wr_ref, br_ref, x1_ref, h_ref, rt_ref):
    d = x_ref.shape[-1]
    gt = gt_ref[...].reshape(-1, d)
    sc = sc_ref[...].reshape(-1, d)
    sh = sh_ref[...].reshape(-1, d)
    x1 = x_ref[...] + gt * p_ref[...]
    x1_ref[...] = x1
    h = _rms(x1, g_ref[...]) * (1.0 + sc) + sh
    h_ref[...] = h.astype(h_ref.dtype)
    h1, h2, h3 = _split3(h)
    w1, w2, w3 = wr_ref[0], wr_ref[1], wr_ref[2]
    lg = (_dot(h1, w1) + (_dot(h1, w2) + _dot(h2, w1))
          + (_dot(h1, w3) + _dot(h2, w2) + _dot(h3, w1)))
    rt_ref[...] = _router_math(lg + br_ref[...])


def _resid_norm_router(x, proj, mod, g, wr3, br, *, nb, nt, tr, per_row, mod_row0, gate_blk, sc_blk, sh_blk,
                       proj_row0, intos, total_rows, name):
    d = x.shape[-1]
    mv = _mod_view(mod, per_row)
    ms = lambda blk: _mod_spec(per_row, tr, d, blk, nt, mod_row0)
    orow = _rows_spec(tr, d, nt, proj_row0)
    return _pcall(
        _resid_norm_router_body, grid=(nb, nt),
        in_specs=[_rows_spec(tr, d, nt, 0), orow, ms(gate_blk), ms(sc_blk), ms(sh_blk),
                  pl.BlockSpec((1, d), lambda b, t: (0, 0)),
                  pl.BlockSpec((3, d, LANES), lambda b, t: (0, 0, 0)),
                  pl.BlockSpec((1, LANES), lambda b, t: (0, 0))],
        args=[x, proj, mv, mv, mv, g, wr3, br],
        outs=[((total_rows, d), F32, orow, intos[0]),
              ((total_rows, d), BF16, orow, intos[1]),
              ((total_rows, LANES), F32, _rows_spec(tr, LANES, nt, proj_row0), intos[2])],
        sem=("parallel", "arbitrary"), name=name)


def _final_body(x1_ref, y1_ref, y2_ref, rt_ref, gt_ref, g_ref, o_ref):
    d = x1_ref.shape[-1]
    gt = gt_ref[...].reshape(-1, d)
    rt = rt_ref[...]
    moe = rt[:, 0:1] * y1_ref[0] + rt[:, 1:2] * y2_ref[0]
    o_ref[...] = _rms(x1_ref[...] + gt * moe, g_ref[...])


def _final(x1, y_tok, route, mod, g, *, nb, nt, tr, per_row, mod_row0, gate_blk, row0, out_rows, name):
    d = x1.shape[-1]
    rb = row0 // tr
    yspec = lambda kk: pl.BlockSpec((1, tr, d), lambda b, t: (kk, rb + b * nt + t, 0))
    return _pcall(
        _final_body, grid=(nb, nt),
        in_specs=[_rows_spec(tr, d, nt, row0), yspec(0), yspec(1), _rows_spec(tr, LANES, nt, row0),
                  _mod_spec(per_row, tr, d, gate_blk, nt, mod_row0),
                  pl.BlockSpec((1, d), lambda b, t: (0, 0))],
        args=[x1, y_tok, y_tok, route, _mod_view(mod, per_row), g],
        outs=[((out_rows, d), F32, _rows_spec(tr, d, nt, 0), None)],
        sem=("parallel", "arbitrary"), name=name)[0]


def _hgrn_tables(c):
    nlev = int(np.log2(c))
    t = np.arange(c)[:, None]
    u = np.arange(c)[None, :]
    mats = [u <= t, u > t]
    masks = []
    for l in range(nlev):
        m = 1 << l
        p = (t // (2 * m)) * (2 * m) + m - 1
        upper = ((t >> l) & 1) == 1
        mats.append(np.where(upper, (u > p) & (u <= t), (u > t) & (u <= p)))
        same = (t >> (l + 1)) == (u >> (l + 1))
        masks.append(same & upper & (((u >> l) & 1) == 0))
    masks.append(t == u)
    w = np.concatenate([m.astype(np.float32) for m in mats], axis=0)
    return jnp.asarray(w, BF16), jnp.asarray(np.stack(masks).astype(np.float32))


def _hgrn_prompt_body(q_ref, f_ref, v_ref, og_ref, lb_ref, gn_ref, w_ref, mk_ref, o_ref, s_ref, st_scr, *,
                      nchunks, nlev):
    c = HG_CHUNK
    t = pl.program_id(2)

    @pl.when(t == 0)
    def _():
        st_scr[...] = jnp.zeros_like(st_scr)

    lb = lb_ref[...]
    gn = gn_ref[...]

    def chunk(ci, carry):
        r0 = pl.multiple_of(ci * c, c)
        rows = pl.ds(r0, c)
        f = lb + (1.0 - lb) * _sigmoid(f_ref[rows, :])
        kk = 1.0 - f
        q = _silu(q_ref[rows, :])
        v = v_ref[rows, :].astype(BF16)
        g = _exact_dot(w_ref[...], jnp.log(f))
        b = g[0:c]
        st = st_scr[...]
        o = _dot_nt((q * jnp.exp(b)).astype(BF16), st.astype(BF16))
        scores = mk_ref[nlev] * _dot_nt(q.astype(BF16), kk.astype(BF16))
        for l in range(nlev):
            x = jnp.exp(g[(2 + l) * c:(3 + l) * c])
            scores = scores + mk_ref[l] * _dot_nt((q * x).astype(BF16), (kk * x).astype(BF16))
        o = o + _dot(scores.astype(BF16), v)
        kw = (kk * jnp.exp(g[c:2 * c])).astype(BF16)
        st_scr[...] = st * jnp.exp(b[c - 1:c, :]) + _dot_tn(v, kw)
        o = _rms(o, gn) * _silu(og_ref[rows, :])
        o_ref[rows, :] = o.astype(o_ref.dtype)
        return carry

    lax.fori_loop(0, nchunks, chunk, 0)

    @pl.when(t == pl.num_programs(2) - 1)
    def _():
        s_ref[0, 0] = st_scr[...].T


def _hgrn_prompt(u, lb, gn, *, nb, seq, nheads, col_q, col_f, col_v, col_og, total_rows, tb):
    c = HG_CHUNK
    nt = seq // tb
    nlev = int(np.log2(c))
    w, mk = _hgrn_tables(c)
    cs = lambda col: pl.BlockSpec((tb, HG_DK), lambda b, h, t: (b * nt + t, col // HG_DK + h))
    hv = pl.BlockSpec((1, HG_DK), lambda b, h, t: (0, h))
    return _pcall(
        functools.partial(_hgrn_prompt_body, nchunks=tb // c, nlev=nlev),
        grid=(nb, nheads, nt),
        in_specs=[cs(col_q), cs(col_f), cs(col_v), cs(col_og), hv, hv,
                  pl.BlockSpec(w.shape, lambda b, h, t: (0, 0)),
                  pl.BlockSpec(mk.shape, lambda b, h, t: (0, 0, 0))],
        args=[u, u, u, u, lb, gn, w, mk],
        outs=[((total_rows, nheads * HG_DK), BF16, pl.BlockSpec((tb, HG_DK), lambda b, h, t: (b * nt + t, h)), None),
              ((nb, nheads, HG_DK, HG_DK), F32, pl.BlockSpec((1, 1, HG_DK, HG_DK), lambda b, h, t: (b, h, 0, 0)), None)],
        sem=("parallel", "parallel", "arbitrary"), name="hgrn_prompt",
        scratch=[pltpu.VMEM((HG_DK, HG_DK), F32)])


def _ssm_expand_tables(n_heads):
    ng = n_heads // SSM_HPG
    w2, w1 = SSM_HPG * LANES, SSM_HPG * SSM_HEADDIM
    e = np.zeros((ng, n_heads, w2 + w1), np.float32)
    for g in range(ng):
        for i in range(SSM_HPG):
            e[g, g * SSM_HPG + i, i * LANES:(i + 1) * LANES] = 1.0
            e[g, g * SSM_HPG + i, w2 + i * SSM_HEADDIM:w2 + (i + 1) * SSM_HEADDIM] = 1.0
    return jnp.asarray(e, BF16)


def _conv_silu(buf_ref, w, bias, r0, n):
    acc = bias
    for j in range(SSM_CONV):
        acc = acc + buf_ref[pl.ds(8 - (SSM_CONV - 1) + j + r0, n), :] * w[j:j + 1, :]
    return _silu(acc)


def _ssd_prompt_body(x_ref, b_ref, c_ref, z_ref, dt_ref, wx_ref, wb_ref, wc_ref, bx_ref, bb_ref, bc_ref,
                     dtb_ref, alog_ref, dsk_ref, e_ref, gn_ref, y_ref, s_ref,
                     ht_scr, cbx, cbb, cbc, *, nchunks):
    L = SSM_CHUNK
    tb = nchunks * L
    w2 = SSM_HPG * LANES
    t = pl.program_id(2)

    @pl.when(t == 0)
    def _():
        ht_scr[...] = jnp.zeros_like(ht_scr)
        cbx[0:8, :] = jnp.zeros((8, cbx.shape[1]), F32)
        cbb[0:8, :] = jnp.zeros((8, cbb.shape[1]), F32)
        cbc[0:8, :] = jnp.zeros((8, cbc.shape[1]), F32)

    cbx[8:8 + tb, :] = x_ref[...]
    cbb[8:8 + tb, :] = b_ref[...]
    cbc[8:8 + tb, :] = c_ref[...]

    e_all = e_ref[0]
    e1 = e_all[:, w2:]
    neg_a = -jnp.exp(alog_ref[...])
    dsk1 = _exact_dot_rhs(dsk_ref[...], e1)
    row = lax.broadcasted_iota(jnp.int32, (L, L), 0)
    col = lax.broadcasted_iota(jnp.int32, (L, L), 1)
    causal = col <= row
    tril = jnp.where(causal, 1.0, 0.0).astype(BF16)
    lane = lax.broadcasted_iota(jnp.int32, (L, LANES), 1)
    lo_half = lane < SSM_HEADDIM

    for ci in range(nchunks):
        r0 = ci * L
        xs = _conv_silu(cbx, wx_ref[...], bx_ref[...], r0, L)
        bm = _conv_silu(cbb, wb_ref[...], bb_ref[...], r0, L).astype(BF16)
        cm = _conv_silu(cbc, wc_ref[...], bc_ref[...], r0, L).astype(BF16)
        dt = _softplus(dt_ref[r0:r0 + L, :] + dtb_ref[...])
        cum = _exact_dot(tril, dt * neg_a)
        cum_e = _exact_dot_rhs(cum, e_all)
        cum1 = cum_e[:, w2:]
        dt1 = _exact_dot_rhs(dt, e1)
        xdt = xs * dt1
        cum_end = cum1[L - 1:L, :]
        ht = ht_scr[...]
        y = _dot(cm, ht.astype(BF16)) * jnp.exp(cum1)
        cb = _dot_nt(cm, bm)
        parts = []
        for j in range(SSM_HPG // 2):
            xp = xdt[:, j * LANES:(j + 1) * LANES]
            acc = None
            for half in range(2):
                i = 2 * j + half
                colb = cum_e[:, i * LANES:(i + 1) * LANES]
                seg = colb - colb.T
                m = (cb * jnp.exp(jnp.where(causal, seg, -jnp.inf))).astype(BF16)
                xh = jnp.where(lo_half if half == 0 else ~lo_half, xp, 0.0).astype(BF16)
                d = _dot(m, xh)
                acc = d if acc is None else acc + d
            parts.append(acc)
        y = y + jnp.concatenate(parts, axis=-1) + dsk1 * xs
        y = y * _silu(z_ref[r0:r0 + L, :])
        y_ref[r0:r0 + L, :] = _rms(y, gn_ref[...]).astype(y_ref.dtype)
        xw = (xdt * jnp.exp(cum_end - cum1)).astype(BF16)
        ht_scr[...] = ht * jnp.exp(cum_end) + _dot_tn(bm, xw)

    cbx[0:8, :] = cbx[tb:tb + 8, :]
    cbb[0:8, :] = cbb[tb:tb + 8, :]
    cbc[0:8, :] = cbc[tb:tb + 8, :]

    @pl.when(t == pl.num_programs(2) - 1)
    def _():
        s_ref[0, 0] = ht_scr[...].T


def _ssd_prompt(u, u_dt, conv_w, conv_b, dt_bias, a_log, d_skip, gn, *, nb, seq, col_z, col_x, total_rows, tb):
    n_heads = u_dt.shape[1]
    ng = n_heads // SSM_HPG
    w1 = SSM_HPG * SSM_HEADDIM
    inner = ng * w1
    nt = seq // tb
    e = _ssm_expand_tables(n_heads)
    xb, bb, cb_, zb = col_x // w1, (col_x + inner) // LANES, (col_x + inner + ng * SSM_STATE) // LANES, col_z // w1
    wide = lambda blk0: pl.BlockSpec((tb, w1), lambda b, g, t: (b * nt + t, blk0 + g))
    thin = lambda blk0: pl.BlockSpec((tb, LANES), lambda b, g, t: (b * nt + t, blk0 + g))
    cw = lambda rows, width, blk0: pl.BlockSpec((rows, width), lambda b, g, t: (0, blk0 + g))
    full = lambda a: pl.BlockSpec(a.shape, lambda b, g, t: (0,) * a.ndim)
    return _pcall(
        functools.partial(_ssd_prompt_body, nchunks=tb // SSM_CHUNK),
        grid=(nb, ng, nt),
        in_specs=[wide(xb), thin(bb), thin(cb_), wide(zb),
                  pl.BlockSpec((tb, n_heads), lambda b, g, t: (b * nt + t, 0)),
                  cw(SSM_CONV, w1, 0), cw(SSM_CONV, LANES, inner // LANES),
                  cw(SSM_CONV, LANES, (inner + ng * SSM_STATE) // LANES),
                  cw(1, w1, 0), cw(1, LANES, inner // LANES), cw(1, LANES, (inner + ng * SSM_STATE) // LANES),
                  full(dt_bias), full(a_log), full(d_skip),
                  pl.BlockSpec((1,) + e.shape[1:], lambda b, g, t: (g, 0, 0)),
                  cw(1, w1, 0)],
        args=[u, u, u, u, u_dt, conv_w, conv_w, conv_w, conv_b, conv_b, conv_b, dt_bias, a_log, d_skip, e, gn],
        outs=[((total_rows, inner), BF16, pl.BlockSpec((tb, w1), lambda b, g, t: (b * nt + t, g)), None),
              ((nb, ng, w1, SSM_STATE), F32, pl.BlockSpec((1, 1, w1, SSM_STATE), lambda b, g, t: (b, g, 0, 0)), None)],
        sem=("parallel", "parallel", "arbitrary"), name="ssd_prompt",
        scratch=[pltpu.VMEM((SSM_STATE, w1), F32), pltpu.VMEM((tb + 8, w1), F32),
                 pltpu.VMEM((tb + 8, LANES), F32), pltpu.VMEM((tb + 8, LANES), F32)])


STEP_TOKENS = 16


def _hgrn_step_body(q_ref, v_ref, og_ref, ft_ref, lbc_ref, gn_ref, s_ref, o_ref, so_ref):
    lbc = lbc_ref[...]
    f = lbc + (1.0 - lbc) * _sigmoid(ft_ref[0])
    kk = 1.0 - f
    q = _silu(q_ref[...])
    v = v_ref[...]
    rows = []
    for j in range(STEP_TOKENS):
        sn = f[:, j:j + 1] * s_ref[j, 0] + kk[:, j:j + 1] * v[j:j + 1, :]
        so_ref[j, 0] = sn
        qj = jnp.broadcast_to(q[j:j + 1, :], (16, HG_DK)).astype(BF16)
        rows.append(_dot(qj, sn.astype(BF16))[0:1, :])
    o = jnp.concatenate(rows, axis=0)
    o_ref[...] = (_rms(o, gn_ref[...]) * _silu(og_ref[...])).astype(o_ref.dtype)


def _hgrn_step(u, f_t, lb_col, gn, state, *, row0, nheads, col_q, col_v, col_og, into):
    n = state.shape[0]
    tb = STEP_TOKENS
    rb = row0 // tb
    cs = lambda col: pl.BlockSpec((tb, HG_DK), lambda b, h: (rb + b, col // HG_DK + h))
    return _pcall(
        _hgrn_step_body, grid=(n // tb, nheads),
        in_specs=[cs(col_q), cs(col_v), cs(col_og),
                  pl.BlockSpec((1, HG_DK, tb), lambda b, h: (b, h, 0)),
                  pl.BlockSpec((HG_DK, 1), lambda b, h: (h, 0)),
                  pl.BlockSpec((1, HG_DK), lambda b, h: (0, h)),
                  pl.BlockSpec((tb, 1, HG_DK, HG_DK), lambda b, h: (b, h, 0, 0))],
        args=[u, u, u, f_t, lb_col, gn, state],
        outs=[(into.shape, into.dtype, pl.BlockSpec((tb, HG_DK), lambda b, h: (rb + b, h)), into),
              (state.shape, F32, pl.BlockSpec((tb, 1, HG_DK, HG_DK), lambda b, h: (b, h, 0, 0)), None)],
        sem=("parallel", "arbitrary"), name="hgrn_step")


def _ssd_prep_body(x_ref, b_ref, c_ref, sx0, sx1, sx2, sb0, sb1, sb2, sc0, sc1, sc2, dt_ref,
                   wx_ref, wb_ref, wc_ref, bx_ref, bb_ref, bc_ref, dtb_ref, alog_ref, e_ref,
                   xs_ref, bm_ref, cm_ref, xdt_ref, da_ref):
    def conv(taps, cur, w_ref, bias_ref):
        w = w_ref[...]
        acc = bias_ref[...] + cur[...] * w[SSM_CONV - 1:SSM_CONV, :]
        for j, tap in enumerate(taps):
            acc = acc + tap[...] * w[j:j + 1, :]
        return _silu(acc)

    xs = conv((sx0, sx1, sx2), x_ref, wx_ref, bx_ref)
    xs_ref[...] = xs
    bm_ref[...] = conv((sb0, sb1, sb2), b_ref, wb_ref, bb_ref)
    cm_ref[...] = conv((sc0, sc1, sc2), c_ref, wc_ref, bc_ref)
    e1 = e_ref[0][:, SSM_HPG * LANES:]
    dt = _softplus(dt_ref[...] + dtb_ref[...])
    xdt_ref[...] = xs * _exact_dot_rhs(dt, e1)
    da_ref[...] = _exact_dot_rhs(jnp.exp(dt * -jnp.exp(alog_ref[...])), e1)


def _ssd_prep(u, u_dt, conv_state2d, conv_w, conv_b, dt_bias, a_log, *, row0, n, col_x):
    n_heads = u_dt.shape[1]
    ng = n_heads // SSM_HPG
    w1 = SSM_HPG * SSM_HEADDIM
    inner = ng * w1
    cdim = inner + 2 * ng * SSM_STATE
    e = _ssm_expand_tables(n_heads)
    rb = row0 // n
    xoff, boff, coff = 0, inner, inner + ng * SSM_STATE
    cur = lambda width, off: pl.BlockSpec((n, width), lambda g: (rb, (col_x + off) // width + g))
    tap = lambda width, off, j: pl.BlockSpec((n, width), lambda g: (0, (j * cdim + off) // width + g))
    cw = lambda rows, width, off: pl.BlockSpec((rows, width), lambda g: (0, off // width + g))
    full = lambda a: pl.BlockSpec(a.shape, lambda g: (0,) * a.ndim)
    taps = [tap(w1, xoff, j) for j in range(3)] + [tap(LANES, boff, j) for j in range(3)] + \
           [tap(LANES, coff, j) for j in range(3)]
    o_wide = pl.BlockSpec((n, w1), lambda g: (0, g))
    o_thin = pl.BlockSpec((n, LANES), lambda g: (0, g))
    return _pcall(
        _ssd_prep_body, grid=(ng,),
        in_specs=[cur(w1, xoff), cur(LANES, boff), cur(LANES, coff)] + taps +
                 [pl.BlockSpec((n, n_heads), lambda g: (rb, 0)),
                  cw(SSM_CONV, w1, xoff), cw(SSM_CONV, LANES, boff), cw(SSM_CONV, LANES, coff),
                  cw(1, w1, xoff), cw(1, LANES, boff), cw(1, LANES, coff),
                  full(dt_bias), full(a_log),
                  pl.BlockSpec((1,) + e.shape[1:], lambda g: (g, 0, 0))],
        args=[u, u, u] + [conv_state2d] * 9 + [u_dt, conv_w, conv_w, conv_w, conv_b, conv_b, conv_b,
                                               dt_bias, a_log, e],
        outs=[((n, inner), F32, o_wide, None), ((n, ng * SSM_STATE), F32, o_thin, None),
              ((n, ng * SSM_STATE), F32, o_thin, None), ((n, inner), F32, o_wide, None),
              ((n, inner), F32, o_wide, None)],
        sem=("arbitrary",), name="ssd_prep")


def _ssd_step_body(xs_ref, bm_ref, cm_ref, z_ref, xdt_ref, da_ref, dsk_ref, gn_ref, s_ref, y_ref, so_ref):
    w1 = SSM_HPG * SSM_HEADDIM
    bm = bm_ref[...]
    cm = cm_ref[...]
    xdt = xdt_ref[0]
    da = da_ref[0]
    rows = []
    for j in range(STEP_TOKENS):
        h = s_ref[j].reshape(w1, SSM_STATE)
        hn = da[:, j:j + 1] * h + xdt[:, j:j + 1] * bm[j:j + 1, :]
        so_ref[j] = hn.reshape(SSM_HPG, SSM_HEADDIM, SSM_STATE)
        cj = jnp.broadcast_to(cm[j:j + 1, :], (16, SSM_STATE)).astype(BF16)
        rows.append(_dot_nt(cj, hn.astype(BF16))[0:1, :])
    y = jnp.concatenate(rows, axis=0) + dsk_ref[...] * xs_ref[...]
    y = y * _silu(z_ref[...])
    y_ref[...] = _rms(y, gn_ref[...]).astype(y_ref.dtype)


def _ssd_step(u, xs, bm, cm, xdt_t, da_t, dsk_full, gn, state, *, row0, col_z, into):
    n, n_heads = state.shape[0], state.shape[1]
    ng = n_heads // SSM_HPG
    w1 = SSM_HPG * SSM_HEADDIM
    tb = STEP_TOKENS
    rb = row0 // tb
    sspec = pl.BlockSpec((tb, SSM_HPG, SSM_HEADDIM, SSM_STATE), lambda b, g: (b, g, 0, 0))
    colspec = pl.BlockSpec((1, w1, tb), lambda b, g: (b, g, 0))
    return _pcall(
        _ssd_step_body, grid=(n // tb, ng),
        in_specs=[pl.BlockSpec((tb, w1), lambda b, g: (b, g)),
                  pl.BlockSpec((tb, SSM_STATE), lambda b, g: (b, g)),
                  pl.BlockSpec((tb, SSM_STATE), lambda b, g: (b, g)),
                  pl.BlockSpec((tb, w1), lambda b, g: (rb + b, col_z // w1 + g)),
                  colspec, colspec,
                  pl.BlockSpec((1, w1), lambda b, g: (0, g)),
                  pl.BlockSpec((1, w1), lambda b, g: (0, g)),
                  sspec],
        args=[xs, bm, cm, u, xdt_t, da_t, dsk_full, gn, state],
        outs=[(into.shape, into.dtype, pl.BlockSpec((tb, w1), lambda b, g: (rb + b, g)), into),
              (state.shape, F32, sspec, None)],
        sem=("parallel", "arbitrary"), name="ssd_step")


def _permute_rows_body(src_ref, x_hbm, o_hbm, sem, *, rows_per_step):
    base = pl.program_id(0) * rows_per_step

    def start(r, carry):
        pltpu.make_async_copy(x_hbm.at[src_ref[base + r]], o_hbm.at[base + r], sem).start()
        return carry

    lax.fori_loop(0, rows_per_step, start, 0)

    def wait(r, carry):
        pltpu.make_async_copy(x_hbm.at[0], o_hbm.at[0], sem).wait()
        return carry

    lax.fori_loop(0, rows_per_step, wait, 0)


def _permute_rows(x, src, *, rows_per_step=MOE_TILE):
    n_out = src.shape[0]
    assert n_out % rows_per_step == 0
    d = x.shape[1]
    x3 = x.reshape(x.shape[0], d // LANES, LANES)
    out = pl.pallas_call(
        functools.partial(_permute_rows_body, rows_per_step=rows_per_step),
        grid_spec=pltpu.PrefetchScalarGridSpec(
            num_scalar_prefetch=1, grid=(n_out // rows_per_step,),
            in_specs=[pl.BlockSpec(memory_space=pl.ANY)],
            out_specs=pl.BlockSpec(memory_space=pl.ANY),
            scratch_shapes=[pltpu.SemaphoreType.DMA(())]),
        out_shape=jax.ShapeDtypeStruct((n_out, d // LANES, LANES), x.dtype),
        compiler_params=_params(("arbitrary",)),
        name="permute_rows",
    )(src, x3)
    return out.reshape(n_out, d)


def _moe_plan(e1, e2, n_tiles):
    n = e1.shape[0]
    tile = MOE_TILE
    eid = jnp.concatenate([e1, e2]).astype(jnp.int32)
    onehot = (eid[:, None] == jnp.arange(N_EXPERTS, dtype=jnp.int32)[None, :]).astype(jnp.int32)
    rank = jnp.sum((jnp.cumsum(onehot, axis=0) - onehot) * onehot, axis=1)
    counts = jnp.sum(onehot, axis=0)
    ptiles = (counts + tile - 1) // tile
    tstart = jnp.cumsum(ptiles) - ptiles
    pos = (tstart * tile)[eid] + rank
    tok = jnp.concatenate([jnp.arange(n, dtype=jnp.int32)] * 2)
    src = jnp.zeros((n_tiles * tile,), jnp.int32).at[pos].set(tok)
    n_used = jnp.sum(ptiles)
    nc = MOE_NCHUNK
    s = jnp.arange(nc * n_tiles, dtype=jnp.int32)
    send = jnp.cumsum(ptiles) * nc
    s_eff = jnp.minimum(s, nc * n_used - 1)
    ex = jnp.searchsorted(send, s_eff, side="right").astype(jnp.int32)
    local = s_eff - (send - ptiles * nc)[ex]
    pt = jnp.maximum(ptiles[ex], 1)
    chunk = local // pt
    tl = tstart[ex] + local % pt
    active = (s < nc * n_used).astype(jnp.int32)
    tl = jnp.where(active == 1, tl, n_tiles)
    wid = ex * nc + chunk
    first = jnp.concatenate([jnp.ones((1,), jnp.int32), (wid[1:] != wid[:-1]).astype(jnp.int32)])
    return pos, src, (tl.astype(jnp.int32), ex, chunk.astype(jnp.int32), first, active)


def _moe_up_body(tl_ref, ex_ref, ck_ref, first_ref, act_ref, x_ref, wg_ref, wu_ref, o_ref, wg_s, wu_s):
    s = pl.program_id(0)

    @pl.when(first_ref[s] == 1)
    def _():
        wg_s[...] = wg_ref[0].astype(BF16)
        wu_s[...] = wu_ref[0].astype(BF16)

    @pl.when(act_ref[s] == 1)
    def _():
        x = x_ref[...]
        o_ref[...] = (_silu(_dot(x, wg_s[...])) * _dot(x, wu_s[...])).astype(o_ref.dtype)

    @pl.when(act_ref[s] == 0)
    def _():
        o_ref[...] = jnp.zeros_like(o_ref)


def _moe_down_body(tl_ref, ex_ref, ck_ref, first_ref, act_ref, a_ref, wd_ref, o_ref, wd_s):
    s = pl.program_id(0)

    @pl.when(first_ref[s] == 1)
    def _():
        wd_s[...] = wd_ref[0].astype(BF16)

    @pl.when(act_ref[s] == 1)
    def _():
        o_ref[...] = _dot(a_ref[...], wd_s[...])

    @pl.when(act_ref[s] == 0)
    def _():
        o_ref[...] = jnp.zeros_like(o_ref)


def _moe_experts(x_sorted, w_gate, w_up, w_down, steps, n_tiles):
    tile = MOE_TILE
    d, de = w_gate.shape[1], w_gate.shape[2]
    nc = MOE_NCHUNK
    fc, dc = de // nc, d // nc
    n_steps = nc * n_tiles
    rows = (n_tiles + 1) * tile
    xmap = lambda s, tl, ex, ck, fi, ac: (jnp.minimum(tl[s], n_tiles - 1), 0)
    wmap = lambda s, tl, ex, ck, fi, ac: (ex[s], 0, ck[s])
    omap = lambda s, tl, ex, ck, fi, ac: (tl[s], ck[s])
    act = pl.pallas_call(
        _moe_up_body,
        grid_spec=pltpu.PrefetchScalarGridSpec(
            num_scalar_prefetch=5, grid=(n_steps,),
            in_specs=[pl.BlockSpec((tile, d), xmap),
                      pl.BlockSpec((1, d, fc), wmap), pl.BlockSpec((1, d, fc), wmap)],
            out_specs=pl.BlockSpec((tile, fc), omap),
            scratch_shapes=[pltpu.VMEM((d, fc), BF16), pltpu.VMEM((d, fc), BF16)]),
        out_shape=jax.ShapeDtypeStruct((rows, de), BF16),
        compiler_params=_params(("arbitrary",)), name="moe_up",
    )(*steps, x_sorted, w_gate, w_up)
    y = pl.pallas_call(
        _moe_down_body,
        grid_spec=pltpu.PrefetchScalarGridSpec(
            num_scalar_prefetch=5, grid=(n_steps,),
            in_specs=[pl.BlockSpec((tile, de), xmap), pl.BlockSpec((1, de, dc), wmap)],
            out_specs=pl.BlockSpec((tile, dc), omap),
            scratch_shapes=[pltpu.VMEM((de, dc), BF16)]),
        out_shape=jax.ShapeDtypeStruct((rows, d), F32),
        compiler_params=_params(("arbitrary",)), name="moe_down",
    )(*steps, act, w_down)
    return y


MM_TM = 1664
MM_TN = 256


def kernel(x_prompt, x_sample, state_hgrn, state_conv, state_ssm, c_prompt, c_sample, lb_params, w_mod, b_mod,
           norm1_g, norm2_g, w_in, hg_norm_g, conv_w, conv_b, dt_bias, a_log, d_skip, ssm_norm_g, w_branch_a,
           w_branch_b, w_out, w_router_group, b_router_group, w_router_expert, b_router_expert, w_gate_e, w_up_e,
           w_down_e, final_g):
    nb, seq, d = x_prompt.shape
    ns = x_sample.shape[0]
    assert w_mod.shape[0] == 1 and x_sample.shape[1] == 1
    n_p = nb * seq
    n = n_p + ns
    nheads = state_hgrn.shape[2]
    hgw = nheads * HG_DK
    n_ssm_heads = state_ssm.shape[2]
    inner = n_ssm_heads * SSM_HEADDIM
    ng = n_ssm_heads // SSM_HPG
    cdim = inner + 2 * ng * SSM_STATE
    col_q, col_f, col_v, col_og = 0, hgw, 2 * hgw, 2 * hgw + d
    col_z = 2 * hgw + 2 * d
    col_x = col_z + inner
    n_main = col_x + cdim
    col_gab = n_main + n_ssm_heads
    row = lambda a: a.reshape(1, -1)

    lb = jnp.cumsum(jax.nn.softmax(lb_params.astype(F32), axis=0), axis=0)[0]

    n_c = ns + nb
    n_c_pad = -(-n_c // 8) * 8
    c_all = jnp.concatenate([c_sample, c_prompt, jnp.zeros((n_c_pad - n_c, d), F32)], axis=0)
    mod = _mm(c_all, w_mod[0], tm=n_c_pad, tn=512, name="mod", a_fn=_silu,
              extras=[(row(b_mod[0]), pl.BlockSpec((1, 512), lambda i, j: (0, j)))],
              epilogue=lambda acc, b: acc + b)
    p_mod = dict(nb=nb, per_row=False, mod_row0=ns)
    s_mod = dict(nb=1, nt=1, tr=ns, per_row=True, mod_row0=0)

    xp2 = x_prompt.reshape(n_p, d)
    xs2 = x_sample.reshape(ns, d)
    h = _norm_mod(xp2, mod, row(norm1_g[0]), nt=seq // 256, tr=256, sc_blk=1, sh_blk=0, into=None, total_rows=n,
                  out_row0=0, name="norm1_prompt", **p_mod)
    h = _norm_mod(xs2, mod, row(norm1_g[0]), sc_blk=1, sh_blk=0, into=h, total_rows=n, out_row0=n_p,
                  name="norm1_sample", **s_mod)

    u = _mm(h, w_in[0], tm=MM_TM, tn=MM_TN, ncols=n_main, name="in_proj")
    u_dt = _mm(h, w_in[0], tm=MM_TM, tn=LANES, col0=n_main, ncols=n_ssm_heads, name="in_proj_dt")
    u_gab = _mm(h, w_in[0][:, col_gab:], tm=MM_TM, tn=MM_TN, name="in_proj_gates")

    gn_h = row(hg_norm_g[0])
    o_a, hg_p = _hgrn_prompt(u, row(lb), gn_h, nb=nb, seq=seq, nheads=nheads, col_q=col_q, col_f=col_f,
                             col_v=col_v, col_og=col_og, total_rows=n, tb=512)
    f_t = u[n_p:, col_f:col_f + hgw].reshape(ns // STEP_TOKENS, STEP_TOKENS, hgw).transpose(0, 2, 1)
    o_a, hg_s = _hgrn_step(u, f_t, lb.reshape(hgw, 1), gn_h, state_hgrn[0], row0=n_p, nheads=nheads,
                           col_q=col_q, col_v=col_v, col_og=col_og, into=o_a)

    gn_s = row(ssm_norm_g[0])
    y_b, ssm_p = _ssd_prompt(u, u_dt, conv_w[0], row(conv_b[0]), row(dt_bias[0]), row(a_log[0]), row(d_skip[0]),
                             gn_s, nb=nb, seq=seq, col_z=col_z, col_x=col_x, total_rows=n, tb=256)
    xs, bm, cm, xdt, da = _ssd_prep(u, u_dt, state_conv[0].reshape(ns, (SSM_CONV - 1) * cdim), conv_w[0],
                                    row(conv_b[0]), row(dt_bias[0]), row(a_log[0]), row0=n_p, n=ns, col_x=col_x)
    cols = lambda a: a.reshape(ns // STEP_TOKENS, STEP_TOKENS, inner).transpose(0, 2, 1)
    dsk_full = jnp.repeat(d_skip[0], SSM_HEADDIM).reshape(1, inner)
    y_b, ssm_s = _ssd_step(u, xs, bm, cm, cols(xdt), cols(da), dsk_full, gn_s, state_ssm[0], row0=n_p,
                           col_z=col_z, into=y_b)
    xbc = u[:, col_x:col_x + cdim]
    conv_p = xbc[:n_p].reshape(nb, seq, cdim)[:, seq - (SSM_CONV - 1):]
    conv_s = jnp.concatenate([state_conv[0][:, 1:], xbc[n_p:, None, :]], axis=1)

    tile_spec = lambda off: pl.BlockSpec((MM_TM, MM_TN), lambda i, j: (i, j + off // MM_TN))
    t_a = _mm(o_a, w_branch_a[0], tm=MM_TM, tn=MM_TN, name="branch_a", extras=[(u_gab, tile_spec(0))],
              epilogue=lambda acc, ga: _sigmoid(ga) * acc)
    half = inner // 2
    p_b = _mm(y_b, w_branch_b[0], tm=MM_TM, tn=MM_TN, k=half, name="branch_b_lo")
    merged = _mm(y_b, w_branch_b[0], tm=MM_TM, tn=MM_TN, k=half, a_kblk=1, b_kblk=1, name="branch_b_hi",
                 extras=[(u_gab, tile_spec(d)), (t_a, tile_spec(0)), (p_b, tile_spec(0))],
                 epilogue=lambda acc, gb, ta, pb: ta + _sigmoid(gb) * (pb + acc), out_dtype=BF16)
    proj = _mm(merged, w_out[0], tm=MM_TM, tn=MM_TN, name="out_proj")

    wr = jnp.concatenate([w_router_group[0], w_router_expert[0],
                          jnp.zeros((d, LANES - N_GROUPS - N_EXPERTS), F32)], axis=1)
    wr3 = jnp.stack(_split3(wr))
    br = jnp.concatenate([b_router_group[0], b_router_expert[0],
                          jnp.zeros((LANES - N_GROUPS - N_EXPERTS,), F32)]).reshape(1, LANES)
    g2n = row(norm2_g[0])
    x1, h2, route = _resid_norm_router(xp2, proj, mod, g2n, wr3, br, nt=seq // 128, tr=128, gate_blk=2, sc_blk=4,
                                       sh_blk=3, proj_row0=0, intos=[None] * 3, total_rows=n,
                                       name="norm2_prompt", **p_mod)
    x1, h2, route = _resid_norm_router(xs2, proj, mod, g2n, wr3, br, gate_blk=2, sc_blk=4, sh_blk=3,
                                       proj_row0=n_p, intos=[x1, h2, route], total_rows=n,
                                       name="norm2_sample", **s_mod)

    e1 = route[:, 2].astype(jnp.int32)
    e2 = route[:, 3].astype(jnp.int32)
    n_tiles = -(-(2 * n + N_EXPERTS * (MOE_TILE - 1)) // MOE_TILE)
    pos, src, steps = _moe_plan(e1, e2, n_tiles)
    x_sorted = _permute_rows(h2, src)
    y_sorted = _moe_experts(x_sorted, w_gate_e[0], w_up_e[0], w_down_e[0], steps, n_tiles)
    y_tok = _permute_rows(y_sorted, pos).reshape(2, n, d)

    fg = row(final_g)
    y_prompt = _final(x1, y_tok, route, mod, fg, nt=seq // 128, tr=128, gate_blk=5, row0=0, out_rows=n_p,
                      name="final_prompt", **p_mod)
    y_sample = _final(x1, y_tok, route, mod, fg, gate_blk=5, row0=n_p, out_rows=ns, name="final_sample", **s_mod)

    return (y_prompt.reshape(nb, seq, d), y_sample.reshape(ns, 1, d),
            hg_p[None], conv_p[None], ssm_p.reshape(1, nb, n_ssm_heads, SSM_HEADDIM, SSM_STATE),
            hg_s[None], conv_s[None], ssm_s[None])
```

```python
import functools

import numpy as np
import jax
import jax.numpy as jnp
from jax import lax
from jax.experimental import pallas as pl
from jax.experimental.pallas import tpu as pltpu

F32 = jnp.float32
BF16 = jnp.bfloat16
EPS = 1e-6

VMEM_LIMIT_BYTES = 56 * 1024 * 1024
LANES = 128

HG_DK = 128
HG_CHUNK = 128
HG_HEADS_PER_STEP = 4
SSM_HEADDIM = 64
SSM_STATE = 128
SSM_HPG = 16
SSM_CHUNK = 128
SSM_CONV = 4
N_GROUPS = 4
EXPERTS_PER_GROUP = 4
N_EXPERTS = 16
MOE_TILE = 256
MOE_NCHUNK = 2


def _params(sem):
    return pltpu.CompilerParams(dimension_semantics=sem, vmem_limit_bytes=VMEM_LIMIT_BYTES)


def _sigmoid(x):
    return 1.0 / (1.0 + jnp.exp(-x))


def _silu(x):
    return x * _sigmoid(x)


def _softplus(x):
    return jnp.maximum(x, 0.0) + jnp.log(1.0 + jnp.exp(-jnp.abs(x)))


def _split3(x):
    p1 = x.astype(BF16)
    r1 = x - p1.astype(F32)
    p2 = r1.astype(BF16)
    r2 = r1 - p2.astype(F32)
    return p1, p2, r2.astype(BF16)


def _dot(a, b):
    return jnp.dot(a, b, preferred_element_type=F32)


def _dot_nt(a, b):
    return lax.dot_general(a, b, (((1,), (1,)), ((), ())), preferred_element_type=F32)


def _dot_tn(a, b):
    return lax.dot_general(a, b, (((0,), (0,)), ((), ())), preferred_element_type=F32)


def _exact_dot(w_bf16, x_f32):
    n = x_f32.shape[-1]
    p1, p2, p3 = _split3(x_f32)
    r = _dot(w_bf16, jnp.concatenate([p1, p2, p3], axis=-1))
    return r[:, :n] + r[:, n:2 * n] + r[:, 2 * n:]


def _exact_dot_rhs(x_f32, w_bf16):
    p1, p2, p3 = _split3(x_f32)
    return _dot(p1, w_bf16) + _dot(p2, w_bf16) + _dot(p3, w_bf16)


def _pcall(body, *, grid, in_specs, args, outs, sem, name, scratch=(), num_prefetch=0):
    n_in = len(args) - num_prefetch
    into = [(i, o[3]) for i, o in enumerate(outs) if o[3] is not None]
    in_specs = list(in_specs) + [pl.BlockSpec(memory_space=pl.ANY)] * len(into)
    all_args = list(args) + [buf for _, buf in into]
    aliases = {len(args) + j: i for j, (i, _) in enumerate(into)}

    def wrapped(*refs):
        keep = refs[:num_prefetch + n_in] + refs[num_prefetch + n_in + len(into):]
        return body(*keep)

    grid_spec = pltpu.PrefetchScalarGridSpec(
        num_scalar_prefetch=num_prefetch, grid=grid, in_specs=in_specs,
        out_specs=[o[2] for o in outs], scratch_shapes=list(scratch))
    res = pl.pallas_call(
        wrapped, grid_spec=grid_spec,
        out_shape=[jax.ShapeDtypeStruct(o[0], o[1]) for o in outs],
        input_output_aliases=aliases, compiler_params=_params(sem), name=name,
    )(*all_args)
    return res


def _mm_body(a_ref, b_ref, *rest, n_extra, a_fn, epilogue):
    extras = rest[:n_extra]
    o_ref = rest[n_extra]
    a = a_ref[...]
    if a_fn is not None:
        a = a_fn(a)
    acc = _dot(a.astype(BF16), b_ref[...].astype(BF16))
    if epilogue is not None:
        acc = epilogue(acc, *[e[...] for e in extras])
    o_ref[...] = acc.astype(o_ref.dtype)


def _mm(a, b, *, tm, tn, name, k=None, a_kblk=0, b_kblk=0, col0=0, ncols=None, extras=(), a_fn=None,
        epilogue=None, out_dtype=F32):
    m = a.shape[0]
    k = a.shape[1] if k is None else k
    ncols = b.shape[1] if ncols is None else ncols
    assert m % tm == 0 and ncols % tn == 0 and col0 % tn == 0
    cb = col0 // tn
    in_specs = [pl.BlockSpec((tm, k), lambda i, j: (i, a_kblk)),
                pl.BlockSpec((k, tn), lambda i, j: (b_kblk, j + cb))]
    in_specs += [s for _, s in extras]
    return pl.pallas_call(
        functools.partial(_mm_body, n_extra=len(extras), a_fn=a_fn, epilogue=epilogue),
        grid=(m // tm, ncols // tn),
        in_specs=in_specs,
        out_specs=pl.BlockSpec((tm, tn), lambda i, j: (i, j)),
        out_shape=jax.ShapeDtypeStruct((m, ncols), out_dtype),
        compiler_params=_params(("parallel", "arbitrary")),
        name=name,
    )(a, b, *[x for x, _ in extras])


def _rms(x, g):
    return x * lax.rsqrt(jnp.mean(x * x, axis=-1, keepdims=True) + EPS) * g


def _mod_view(mod, per_row):
    return mod if per_row else mod.reshape(mod.shape[0], 1, mod.shape[1])


def _mod_spec(per_row, tr, d, col_blk, nt, row0):
    if per_row:
        rb = row0 // tr
        return pl.BlockSpec((tr, d), lambda b, t: (rb + t, col_blk))
    return pl.BlockSpec((1, 1, d), lambda b, t: (row0 + b, 0, col_blk))


def _rows_spec(tr, d, nt, row0):
    rb = row0 // tr
    return pl.BlockSpec((tr, d), lambda b, t: (rb + b * nt + t, 0))


def _norm_mod_body(x_ref, sc_ref, sh_ref, g_ref, o_ref):
    d = x_ref.shape[-1]
    sc = sc_ref[...].reshape(-1, d)
    sh = sh_ref[...].reshape(-1, d)
    o_ref[...] = (_rms(x_ref[...], g_ref[...]) * (1.0 + sc) + sh).astype(o_ref.dtype)


def _norm_mod(x, mod, g, *, nb, nt, tr, per_row, mod_row0, sc_blk, sh_blk, into, total_rows, out_row0, name):
    d = x.shape[-1]
    mv = _mod_view(mod, per_row)
    return _pcall(
        _norm_mod_body, grid=(nb, nt),
        in_specs=[_rows_spec(tr, d, nt, 0),
                  _mod_spec(per_row, tr, d, sc_blk, nt, mod_row0),
                  _mod_spec(per_row, tr, d, sh_blk, nt, mod_row0),
                  pl.BlockSpec((1, d), lambda b, t: (0, 0))],
        args=[x, mv, mv, g],
        outs=[((total_rows, d), BF16, _rows_spec(tr, d, nt, out_row0), into)],
        sem=("parallel", "arbitrary"), name=name)[0]


def _router_math(lg):
    lane = lax.broadcasted_iota(jnp.int32, lg.shape, 1)
    neg = jnp.float32(-jnp.inf)
    gm = jnp.where(lane < N_GROUPS, lg, neg)
    gmax = jnp.max(gm, axis=-1, keepdims=True)
    gidx = jnp.min(jnp.where(gm == gmax, lane, LANES), axis=-1, keepdims=True)
    gsum = jnp.sum(jnp.where(lane < N_GROUPS, jnp.exp(gm - gmax), 0.0), axis=-1, keepdims=True)
    p_group = 1.0 / gsum
    lo = N_GROUPS + EXPERTS_PER_GROUP * gidx
    em = jnp.where((lane >= lo) & (lane < lo + EXPERTS_PER_GROUP), lg, neg)
    m1 = jnp.max(em, axis=-1, keepdims=True)
    i1 = jnp.min(jnp.where(em == m1, lane, LANES), axis=-1, keepdims=True)
    em2 = jnp.where(lane == i1, neg, em)
    m2 = jnp.max(em2, axis=-1, keepdims=True)
    i2 = jnp.min(jnp.where(em2 == m2, lane, LANES), axis=-1, keepdims=True)
    r = jnp.exp(m2 - m1)
    w1 = p_group / (1.0 + r)
    w2 = p_group * r / (1.0 + r)
    e1 = (i1 - N_GROUPS).astype(F32)
    e2 = (i2 - N_GROUPS).astype(F32)
    out = jnp.where(lane == 0, w1, 0.0)
    out = jnp.where(lane == 1, w2, out)
    out = jnp.where(lane == 2, e1, out)
    out = jnp.where(lane == 3, e2, out)
    return out


def _resid_norm_router_body(x_ref, p_ref, gt_ref, sc_ref, sh_ref, g_ref, wr_ref, br_ref, x1_ref, h_ref, rt_ref):
    d = x_ref.shape[-1]
    gt = gt_ref[...].reshape(-1, d)
    sc = sc_ref[...].reshape(-1, d)
    sh = sh_ref[...].reshape(-1, d)
    x1 = x_ref[...] + gt * p_ref[...]
    x1_ref[...] = x1
    h = _rms(x1, g_ref[...]) * (1.0 + sc) + sh
    h_ref[...] = h.astype(h_ref.dtype)
    h1, h2, h3 = _split3(h)
    w1, w2, w3 = wr_ref[0], wr_ref[1], wr_ref[2]
    lg = (_dot(h1, w1) + (_dot(h1, w2) + _dot(h2, w1))
          + (_dot(h1, w3) + _dot(h2, w2) + _dot(h3, w1)))
    rt_ref[...] = _router_math(lg + br_ref[...])


def _resid_norm_router(x, proj, mod, g, wr3, br, *, nb, nt, tr, per_row, mod_row0, gate_blk, sc_blk, sh_blk,
                       proj_row0, intos, total_rows, name):
    d = x.shape[-1]
    mv = _mod_view(mod, per_row)
    ms = lambda blk: _mod_spec(per_row, tr, d, blk, nt, mod_row0)
    orow = _rows_spec(tr, d, nt, proj_row0)
    return _pcall(
        _resid_norm_router_body, grid=(nb, nt),
        in_specs=[_rows_spec(tr, d, nt, 0), orow, ms(gate_blk), ms(sc_blk), ms(sh_blk),
                  pl.BlockSpec((1, d), lambda b, t: (0, 0)),
                  pl.BlockSpec((3, d, LANES), lambda b, t: (0, 0, 0)),
                  pl.BlockSpec((1, LANES), lambda b, t: (0, 0))],
        args=[x, proj, mv, mv, mv, g, wr3, br],
        outs=[((total_rows, d), F32, orow, intos[0]),
              ((total_rows, d), F32, orow, intos[1]),
              ((total_rows, LANES), F32, _rows_spec(tr, LANES, nt, proj_row0), intos[2])],
        sem=("parallel", "arbitrary"), name=name)


def _row_gather_start(src_hbm, idx_ref, idx0, dst, sem, n):
    def body(r, carry):
        pltpu.make_async_copy(src_hbm.at[pl.ds(idx_ref[idx0 + r], 1), :], dst.at[pl.ds(r, 1), :], sem).start()
        return carry

    lax.fori_loop(0, n, body, 0, unroll=8)


def _row_gather_wait(src_hbm, dst, sem, n):
    def body(r, carry):
        pltpu.make_async_copy(src_hbm.at[pl.ds(0, 1), :], dst.at[pl.ds(r, 1), :], sem).wait()
        return carry

    lax.fori_loop(0, n, body, 0, unroll=8)


def _final_body(pos_ref, x1_ref, rt_ref, gt_ref, g_ref, y_hbm, o_ref, ybuf, sem, *, tr, tok0, n_tok):
    d = x1_ref.shape[-1]
    i = pl.program_id(0)
    slot = lax.rem(i, 2)

    def start(step, sl):
        for k in range(2):
            _row_gather_start(y_hbm, pos_ref, k * n_tok + tok0 + step * tr, ybuf.at[sl, k], sem.at[sl], tr)

    @pl.when(i == 0)
    def _():
        start(0, 0)

    @pl.when(i + 1 < pl.num_programs(0))
    def _():
        start(i + 1, 1 - slot)

    for k in range(2):
        _row_gather_wait(y_hbm, ybuf.at[slot, k], sem.at[slot], tr)
    gt = gt_ref[...].reshape(-1, d)
    rt = rt_ref[...]
    moe = rt[:, 0:1] * ybuf[slot, 0] + rt[:, 1:2] * ybuf[slot, 1]
    o_ref[...] = _rms(x1_ref[...] + gt * moe, g_ref[...])


def _final(x1, y_sorted, pos, route, mod, g, *, nb, nt, tr, per_row, mod_row0, gate_blk, row0, out_rows, name):
    d = x1.shape[-1]
    n_tok = x1.shape[0]
    rb = row0 // tr
    if per_row:
        mspec = pl.BlockSpec((tr, d), lambda i, pos_: (mod_row0 // tr + i, gate_blk))
    else:
        mspec = pl.BlockSpec((1, 1, d), lambda i, pos_: (mod_row0 + i // nt, 0, gate_blk))
    return _pcall(
        functools.partial(_final_body, tr=tr, tok0=row0, n_tok=n_tok), grid=(nb * nt,),
        in_specs=[pl.BlockSpec((tr, d), lambda i, pos_: (rb + i, 0)),
                  pl.BlockSpec((tr, LANES), lambda i, pos_: (rb + i, 0)),
                  mspec,
                  pl.BlockSpec((1, d), lambda i, pos_: (0, 0)),
                  pl.BlockSpec(memory_space=pl.ANY)],
        args=[pos, x1, route, _mod_view(mod, per_row), g, y_sorted],
        outs=[((out_rows, d), F32, pl.BlockSpec((tr, d), lambda i, pos_: (i, 0)), None)],
        sem=("arbitrary",), name=name, num_prefetch=1,
        scratch=[pltpu.VMEM((2, 2, tr, d), F32), pltpu.SemaphoreType.DMA((2,))])[0]


HG_MXU_LEVELS = 3


def _hgrn_tables(c):
    nlev = int(np.log2(c))
    t = np.arange(c)[:, None]
    u = np.arange(c)[None, :]
    mats = [u <= t]
    masks = []
    for l in range(nlev):
        m = 1 << l
        p = (t // (2 * m)) * (2 * m) + m - 1
        upper = ((t >> l) & 1) == 1
        if l < HG_MXU_LEVELS:
            mats.append(np.where(upper, (u > p) & (u <= t), (u > t) & (u <= p)))
        same = (t >> (l + 1)) == (u >> (l + 1))
        masks.append(same & upper & (((u >> l) & 1) == 0))
    masks.append(t == u)
    w = np.concatenate([m.astype(np.float32) for m in mats], axis=0)
    return jnp.asarray(w, BF16), jnp.asarray(np.stack(masks).astype(np.float32))


def _hgrn_prompt_body(q_ref, f_ref, v_ref, og_ref, lb_ref, gn_ref, w_ref, mk_ref, o_ref, s_ref, st_scr, *,
                      nchunks, nlev):
    c = HG_CHUNK
    hp = st_scr.shape[0]
    t = pl.program_id(2)

    @pl.when(t == 0)
    def _():
        st_scr[...] = jnp.zeros_like(st_scr)

    lb = lb_ref[...]
    gn = gn_ref[...]
    trow = lax.broadcasted_iota(jnp.int32, (c, HG_DK), 0)
    upper = [((trow >> l) & 1) == 1 for l in range(HG_MXU_LEVELS, nlev)]

    def chunk(ci, carry):
        r0 = pl.multiple_of(ci * c, c)
        rows = pl.ds(r0, c)
        for i in range(hp):
            ln = slice(i * HG_DK, (i + 1) * HG_DK)
            f = lb[:, ln] + (1.0 - lb[:, ln]) * _sigmoid(f_ref[rows, ln])
            g = _exact_dot(w_ref[...], jnp.log2(f))
            kk = 1.0 - f
            q = _silu(q_ref[rows, ln])
            v = v_ref[rows, ln].astype(BF16)
            b = g[0:c]
            b_end = b[c - 1:c, :]
            st = st_scr[i]
            o = _dot_nt((q * jnp.exp2(b)).astype(BF16), st.astype(BF16))
            scores = mk_ref[nlev] * _dot_nt(q.astype(BF16), kk.astype(BF16))
            for l in range(nlev):
                if l < HG_MXU_LEVELS:
                    x = jnp.exp2(g[(1 + l) * c:(2 + l) * c])
                else:
                    m = 1 << l
                    b3 = b.reshape(c // (2 * m), 2 * m, HG_DK)
                    d = (b3 - b3[:, m - 1:m, :]).reshape(c, HG_DK)
                    x = jnp.exp2(jnp.where(upper[l - HG_MXU_LEVELS], d, -d))
                scores = scores + mk_ref[l] * _dot_nt((q * x).astype(BF16), (kk * x).astype(BF16))
            o = o + _dot(scores.astype(BF16), v)
            kw = (kk * jnp.exp2(b_end - b)).astype(BF16)
            st_scr[i] = st * jnp.exp2(b_end) + _dot_tn(v, kw)
            o = _rms(o, gn[:, ln]) * _silu(og_ref[rows, ln])
            o_ref[rows, ln] = o.astype(o_ref.dtype)
        return carry

    lax.fori_loop(0, nchunks, chunk, 0)

    @pl.when(t == pl.num_programs(2) - 1)
    def _():
        for i in range(hp):
            s_ref[0, i] = st_scr[i].T


def _hgrn_prompt(u, lb, gn, *, nb, seq, nheads, col_q, col_f, col_v, col_og, total_rows, tb):
    c = HG_CHUNK
    nt = seq // tb
    nlev = int(np.log2(c))
    w, mk = _hgrn_tables(c)
    hp = HG_HEADS_PER_STEP
    wd = hp * HG_DK
    cs = lambda col: pl.BlockSpec((tb, wd), lambda b, h, t: (b * nt + t, col // wd + h))
    hv = pl.BlockSpec((1, wd), lambda b, h, t: (0, h))
    return _pcall(
        functools.partial(_hgrn_prompt_body, nchunks=tb // c, nlev=nlev),
        grid=(nb, nheads // hp, nt),
        in_specs=[cs(col_q), cs(col_f), cs(col_v), cs(col_og), hv, hv,
                  pl.BlockSpec(w.shape, lambda b, h, t: (0, 0)),
                  pl.BlockSpec(mk.shape, lambda b, h, t: (0, 0, 0))],
        args=[u, u, u, u, lb, gn, w, mk],
        outs=[((total_rows, nheads * HG_DK), BF16, pl.BlockSpec((tb, wd), lambda b, h, t: (b * nt + t, h)), None),
              ((nb, nheads, HG_DK, HG_DK), F32, pl.BlockSpec((1, hp, HG_DK, HG_DK), lambda b, h, t: (b, h, 0, 0)), None)],
        sem=("parallel", "parallel", "arbitrary"), name="hgrn_prompt",
        scratch=[pltpu.VMEM((hp, HG_DK, HG_DK), F32)])


def _ssm_expand_tables(n_heads):
    ng = n_heads // SSM_HPG
    w2, w1 = SSM_HPG * LANES, SSM_HPG * SSM_HEADDIM
    e = np.zeros((ng, n_heads, w2 + w1), np.float32)
    for g in range(ng):
        for i in range(SSM_HPG):
            e[g, g * SSM_HPG + i, i * LANES:(i + 1) * LANES] = 1.0
            e[g, g * SSM_HPG + i, w2 + i * SSM_HEADDIM:w2 + (i + 1) * SSM_HEADDIM] = 1.0
    return jnp.asarray(e, BF16)


def _conv_silu(buf_ref, w, bias, r0, n):
    acc = bias
    for j in range(SSM_CONV):
        acc = acc + buf_ref[pl.ds(8 - (SSM_CONV - 1) + j + r0, n), :] * w[j:j + 1, :]
    return _silu(acc)


def _ssd_prompt_body(x_ref, b_ref, c_ref, z_ref, dt_ref, wx_ref, wb_ref, wc_ref, bx_ref, bb_ref, bc_ref,
                     dtb_ref, alog_ref, dsk_ref, e_ref, gn_ref, y_ref, s_ref,
                     ht_scr, cbx, cbb, cbc, *, nchunks):
    L = SSM_CHUNK
    tb = nchunks * L
    w2 = SSM_HPG * LANES
    t = pl.program_id(2)

    @pl.when(t == 0)
    def _():
        ht_scr[...] = jnp.zeros_like(ht_scr)
        cbx[0:8, :] = jnp.zeros((8, cbx.shape[1]), F32)
        cbb[0:8, :] = jnp.zeros((8, cbb.shape[1]), F32)
        cbc[0:8, :] = jnp.zeros((8, cbc.shape[1]), F32)

    cbx[8:8 + tb, :] = x_ref[...]
    cbb[8:8 + tb, :] = b_ref[...]
    cbc[8:8 + tb, :] = c_ref[...]

    e_all = e_ref[0]
    e1 = e_all[:, w2:]
    neg_a = -jnp.exp(alog_ref[...])
    dsk1 = _exact_dot_rhs(dsk_ref[...], e1)
    row = lax.broadcasted_iota(jnp.int32, (L, L), 0)
    col = lax.broadcasted_iota(jnp.int32, (L, L), 1)
    causal = col <= row
    tril = jnp.where(causal, 1.0, 0.0).astype(BF16)
    lane = lax.broadcasted_iota(jnp.int32, (L, LANES), 1)
    lo_half = lane < SSM_HEADDIM

    for ci in range(nchunks):
        r0 = ci * L
        xs = _conv_silu(cbx, wx_ref[...], bx_ref[...], r0, L)
        bm = _conv_silu(cbb, wb_ref[...], bb_ref[...], r0, L).astype(BF16)
        cm = _conv_silu(cbc, wc_ref[...], bc_ref[...], r0, L).astype(BF16)
        dt = _softplus(dt_ref[r0:r0 + L, :] + dtb_ref[...])
        cum = _exact_dot(tril, dt * neg_a)
        cum_e = _exact_dot_rhs(cum, e_all)
        cum1 = cum_e[:, w2:]
        dt1 = _exact_dot_rhs(dt, e1)
        xdt = xs * dt1
        cum_end = cum1[L - 1:L, :]
        ht = ht_scr[...]
        y = _dot(cm, ht.astype(BF16)) * jnp.exp(cum1)
        cb = _dot_nt(cm, bm)
        parts = []
        for j in range(SSM_HPG // 2):
            xp = xdt[:, j * LANES:(j + 1) * LANES]
            acc = None
            for half in range(2):
                i = 2 * j + half
                colb = cum_e[:, i * LANES:(i + 1) * LANES]
                seg = colb - colb.T
                m = (cb * jnp.exp(jnp.where(causal, seg, -jnp.inf))).astype(BF16)
                xh = jnp.where(lo_half if half == 0 else ~lo_half, xp, 0.0).astype(BF16)
                d = _dot(m, xh)
                acc = d if acc is None else acc + d
            parts.append(acc)
        y = y + jnp.concatenate(parts, axis=-1) + dsk1 * xs
        y = y * _silu(z_ref[r0:r0 + L, :])
        y_ref[r0:r0 + L, :] = _rms(y, gn_ref[...]).astype(y_ref.dtype)
        xw = (xdt * jnp.exp(cum_end - cum1)).astype(BF16)
        ht_scr[...] = ht * jnp.exp(cum_end) + _dot_tn(bm, xw)

    cbx[0:8, :] = cbx[tb:tb + 8, :]
    cbb[0:8, :] = cbb[tb:tb + 8, :]
    cbc[0:8, :] = cbc[tb:tb + 8, :]

    @pl.when(t == pl.num_programs(2) - 1)
    def _():
        s_ref[0, 0] = ht_scr[...].T


def _ssd_prompt(u, u_dt, conv_w, conv_b, dt_bias, a_log, d_skip, gn, *, nb, seq, col_z, col_x, total_rows, tb):
    n_heads = u_dt.shape[1]
    ng = n_heads // SSM_HPG
    w1 = SSM_HPG * SSM_HEADDIM
    inner = ng * w1
    nt = seq // tb
    e = _ssm_expand_tables(n_heads)
    xb, bb, cb_, zb = col_x // w1, (col_x + inner) // LANES, (col_x + inner + ng * SSM_STATE) // LANES, col_z // w1
    wide = lambda blk0: pl.BlockSpec((tb, w1), lambda b, g, t: (b * nt + t, blk0 + g))
    thin = lambda blk0: pl.BlockSpec((tb, LANES), lambda b, g, t: (b * nt + t, blk0 + g))
    cw = lambda rows, width, blk0: pl.BlockSpec((rows, width), lambda b, g, t: (0, blk0 + g))
    full = lambda a: pl.BlockSpec(a.shape, lambda b, g, t: (0,) * a.ndim)
    return _pcall(
        functools.partial(_ssd_prompt_body, nchunks=tb // SSM_CHUNK),
        grid=(nb, ng, nt),
        in_specs=[wide(xb), thin(bb), thin(cb_), wide(zb),
                  pl.BlockSpec((tb, n_heads), lambda b, g, t: (b * nt + t, 0)),
                  cw(SSM_CONV, w1, 0), cw(SSM_CONV, LANES, inner // LANES),
                  cw(SSM_CONV, LANES, (inner + ng * SSM_STATE) // LANES),
                  cw(1, w1, 0), cw(1, LANES, inner // LANES), cw(1, LANES, (inner + ng * SSM_STATE) // LANES),
                  full(dt_bias), full(a_log), full(d_skip),
                  pl.BlockSpec((1,) + e.shape[1:], lambda b, g, t: (g, 0, 0)),
                  cw(1, w1, 0)],
        args=[u, u, u, u, u_dt, conv_w, conv_w, conv_w, conv_b, conv_b, conv_b, dt_bias, a_log, d_skip, e, gn],
        outs=[((total_rows, inner), BF16, pl.BlockSpec((tb, w1), lambda b, g, t: (b * nt + t, g)), None),
              ((nb, ng, w1, SSM_STATE), F32, pl.BlockSpec((1, 1, w1, SSM_STATE), lambda b, g, t: (b, g, 0, 0)), None)],
        sem=("parallel", "parallel", "arbitrary"), name="ssd_prompt",
        scratch=[pltpu.VMEM((SSM_STATE, w1), F32), pltpu.VMEM((tb + 8, w1), F32),
                 pltpu.VMEM((tb + 8, LANES), F32), pltpu.VMEM((tb + 8, LANES), F32)])


STEP_TOKENS = 16


def _hgrn_step_body(q_ref, v_ref, og_ref, ft_ref, lbc_ref, gn_ref, s_ref, o_ref, so_ref):
    lbc = lbc_ref[...]
    f = lbc + (1.0 - lbc) * _sigmoid(ft_ref[0])
    kk = 1.0 - f
    q = _silu(q_ref[...])
    v = v_ref[...]
    rows = []
    for j in range(STEP_TOKENS):
        sn = f[:, j:j + 1] * s_ref[j, 0] + kk[:, j:j + 1] * v[j:j + 1, :]
        so_ref[j, 0] = sn
        qj = jnp.broadcast_to(q[j:j + 1, :], (16, HG_DK)).astype(BF16)
        rows.append(_dot(qj, sn.astype(BF16))[0:1, :])
    o = jnp.concatenate(rows, axis=0)
    o_ref[...] = (_rms(o, gn_ref[...]) * _silu(og_ref[...])).astype(o_ref.dtype)


def _hgrn_step(u, f_t, lb_col, gn, state, *, row0, nheads, col_q, col_v, col_og, into):
    n = state.shape[0]
    tb = STEP_TOKENS
    rb = row0 // tb
    cs = lambda col: pl.BlockSpec((tb, HG_DK), lambda b, h: (rb + b, col // HG_DK + h))
    return _pcall(
        _hgrn_step_body, grid=(n // tb, nheads),
        in_specs=[cs(col_q), cs(col_v), cs(col_og),
                  pl.BlockSpec((1, HG_DK, tb), lambda b, h: (b, h, 0)),
                  pl.BlockSpec((HG_DK, 1), lambda b, h: (h, 0)),
                  pl.BlockSpec((1, HG_DK), lambda b, h: (0, h)),
                  pl.BlockSpec((tb, 1, HG_DK, HG_DK), lambda b, h: (b, h, 0, 0))],
        args=[u, u, u, f_t, lb_col, gn, state],
        outs=[(into.shape, into.dtype, pl.BlockSpec((tb, HG_DK), lambda b, h: (rb + b, h)), into),
              (state.shape, F32, pl.BlockSpec((tb, 1, HG_DK, HG_DK), lambda b, h: (b, h, 0, 0)), None)],
        sem=("parallel", "arbitrary"), name="hgrn_step")


def _ssd_prep_body(x_ref, b_ref, c_ref, sx0, sx1, sx2, sb0, sb1, sb2, sc0, sc1, sc2, dt_ref,
                   wx_ref, wb_ref, wc_ref, bx_ref, bb_ref, bc_ref, dtb_ref, alog_ref, e_ref,
                   xs_ref, bm_ref, cm_ref, xdt_ref, da_ref):
    def conv(taps, cur, w_ref, bias_ref):
        w = w_ref[...]
        acc = bias_ref[...] + cur[...] * w[SSM_CONV - 1:SSM_CONV, :]
        for j, tap in enumerate(taps):
            acc = acc + tap[...] * w[j:j + 1, :]
        return _silu(acc)

    xs = conv((sx0, sx1, sx2), x_ref, wx_ref, bx_ref)
    xs_ref[...] = xs
    bm_ref[...] = conv((sb0, sb1, sb2), b_ref, wb_ref, bb_ref)
    cm_ref[...] = conv((sc0, sc1, sc2), c_ref, wc_ref, bc_ref)
    e1 = e_ref[0][:, SSM_HPG * LANES:]
    dt = _softplus(dt_ref[...] + dtb_ref[...])
    xdt_ref[...] = xs * _exact_dot_rhs(dt, e1)
    da_ref[...] = _exact_dot_rhs(jnp.exp(dt * -jnp.exp(alog_ref[...])), e1)


def _ssd_prep(u, u_dt, conv_state2d, conv_w, conv_b, dt_bias, a_log, *, row0, n, col_x):
    n_heads = u_dt.shape[1]
    ng = n_heads // SSM_HPG
    w1 = SSM_HPG * SSM_HEADDIM
    inner = ng * w1
    cdim = inner + 2 * ng * SSM_STATE
    e = _ssm_expand_tables(n_heads)
    rb = row0 // n
    xoff, boff, coff = 0, inner, inner + ng * SSM_STATE
    cur = lambda width, off: pl.BlockSpec((n, width), lambda g: (rb, (col_x + off) // width + g))
    tap = lambda width, off, j: pl.BlockSpec((n, width), lambda g: (0, (j * cdim + off) // width + g))
    cw = lambda rows, width, off: pl.BlockSpec((rows, width), lambda g: (0, off // width + g))
    full = lambda a: pl.BlockSpec(a.shape, lambda g: (0,) * a.ndim)
    taps = [tap(w1, xoff, j) for j in range(3)] + [tap(LANES, boff, j) for j in range(3)] + \
           [tap(LANES, coff, j) for j in range(3)]
    o_wide = pl.BlockSpec((n, w1), lambda g: (0, g))
    o_thin = pl.BlockSpec((n, LANES), lambda g: (0, g))
    return _pcall(
        _ssd_prep_body, grid=(ng,),
        in_specs=[cur(w1, xoff), cur(LANES, boff), cur(LANES, coff)] + taps +
                 [pl.BlockSpec((n, n_heads), lambda g: (rb, 0)),
                  cw(SSM_CONV, w1, xoff), cw(SSM_CONV, LANES, boff), cw(SSM_CONV, LANES, coff),
                  cw(1, w1, xoff), cw(1, LANES, boff), cw(1, LANES, coff),
                  full(dt_bias), full(a_log),
                  pl.BlockSpec((1,) + e.shape[1:], lambda g: (g, 0, 0))],
        args=[u, u, u] + [conv_state2d] * 9 + [u_dt, conv_w, conv_w, conv_w, conv_b, conv_b, conv_b,
                                               dt_bias, a_log, e],
        outs=[((n, inner), F32, o_wide, None), ((n, ng * SSM_STATE), F32, o_thin, None),
              ((n, ng * SSM_STATE), F32, o_thin, None), ((n, inner), F32, o_wide, None),
              ((n, inner), F32, o_wide, None)],
        sem=("arbitrary",), name="ssd_prep")


def _ssd_step_body(xs_ref, bm_ref, cm_ref, z_ref, xdt_ref, da_ref, dsk_ref, gn_ref, s_ref, y_ref, so_ref):
    w1 = SSM_HPG * SSM_HEADDIM
    bm = bm_ref[...]
    cm = cm_ref[...]
    xdt = xdt_ref[0]
    da = da_ref[0]
    rows = []
    for j in range(STEP_TOKENS):
        h = s_ref[j].reshape(w1, SSM_STATE)
        hn = da[:, j:j + 1] * h + xdt[:, j:j + 1] * bm[j:j + 1, :]
        so_ref[j] = hn.reshape(SSM_HPG, SSM_HEADDIM, SSM_STATE)
        cj = jnp.broadcast_to(cm[j:j + 1, :], (16, SSM_STATE)).astype(BF16)
        rows.append(_dot_nt(cj, hn.astype(BF16))[0:1, :])
    y = jnp.concatenate(rows, axis=0) + dsk_ref[...] * xs_ref[...]
    y = y * _silu(z_ref[...])
    y_ref[...] = _rms(y, gn_ref[...]).astype(y_ref.dtype)


def _ssd_step(u, xs, bm, cm, xdt_t, da_t, dsk_full, gn, state, *, row0, col_z, into):
    n, n_heads = state.shape[0], state.shape[1]
    ng = n_heads // SSM_HPG
    w1 = SSM_HPG * SSM_HEADDIM
    tb = STEP_TOKENS
    rb = row0 // tb
    sspec = pl.BlockSpec((tb, SSM_HPG, SSM_HEADDIM, SSM_STATE), lambda b, g: (b, g, 0, 0))
    colspec = pl.BlockSpec((1, w1, tb), lambda b, g: (b, g, 0))
    return _pcall(
        _ssd_step_body, grid=(n // tb, ng),
        in_specs=[pl.BlockSpec((tb, w1), lambda b, g: (b, g)),
                  pl.BlockSpec((tb, SSM_STATE), lambda b, g: (b, g)),
                  pl.BlockSpec((tb, SSM_STATE), lambda b, g: (b, g)),
                  pl.BlockSpec((tb, w1), lambda b, g: (rb + b, col_z // w1 + g)),
                  colspec, colspec,
                  pl.BlockSpec((1, w1), lambda b, g: (0, g)),
                  pl.BlockSpec((1, w1), lambda b, g: (0, g)),
                  sspec],
        args=[xs, bm, cm, u, xdt_t, da_t, dsk_full, gn, state],
        outs=[(into.shape, into.dtype, pl.BlockSpec((tb, w1), lambda b, g: (rb + b, g)), into),
              (state.shape, F32, sspec, None)],
        sem=("parallel", "arbitrary"), name="ssd_step")


def _moe_plan(e1, e2, n_tiles):
    n = e1.shape[0]
    tile = MOE_TILE
    eid = jnp.concatenate([e1, e2]).astype(jnp.int32)
    onehot = (eid[:, None] == jnp.arange(N_EXPERTS, dtype=jnp.int32)[None, :]).astype(jnp.int32)
    rank = jnp.sum((jnp.cumsum(onehot, axis=0) - onehot) * onehot, axis=1)
    counts = jnp.sum(onehot, axis=0)
    ptiles = (counts + tile - 1) // tile
    tstart = jnp.cumsum(ptiles) - ptiles
    pos = (tstart * tile)[eid] + rank
    tok = jnp.concatenate([jnp.arange(n, dtype=jnp.int32)] * 2)
    src = jnp.zeros((n_tiles * tile,), jnp.int32).at[pos].set(tok)
    n_used = jnp.sum(ptiles)
    nc = MOE_NCHUNK
    s = jnp.arange(nc * n_tiles, dtype=jnp.int32)
    send = jnp.cumsum(ptiles) * nc
    s_eff = jnp.minimum(s, nc * n_used - 1)
    ex = jnp.searchsorted(send, s_eff, side="right").astype(jnp.int32)
    local = s_eff - (send - ptiles * nc)[ex]
    pt = jnp.maximum(ptiles[ex], 1)
    chunk = local // pt
    tl = tstart[ex] + local % pt
    active = (s < nc * n_used).astype(jnp.int32)
    idle = s - nc * n_used
    tl = jnp.where(active == 1, tl, n_used + idle // nc)
    ock = jnp.where(active == 1, chunk, idle % nc)
    wid = ex * nc + chunk
    first = jnp.concatenate([jnp.ones((1,), jnp.int32), (wid[1:] != wid[:-1]).astype(jnp.int32)])
    i32 = lambda a: a.astype(jnp.int32)
    return i32(pos), src, (i32(tl), i32(ex), i32(chunk), i32(ock), first, active)


def _moe_up_body(tl_ref, ex_ref, wck_ref, ock_ref, first_ref, act_ref, src_ref, x_hbm, wg_ref, wu_ref, o_ref,
                 wg_s, wu_s, xbuf, sem):
    s = pl.program_id(0)
    slot = lax.rem(s, 2)
    tile = xbuf.shape[1]

    def start(step, sl):
        _row_gather_start(x_hbm, src_ref, tl_ref[step] * tile, xbuf.at[sl], sem.at[sl], tile)

    @pl.when((s == 0) & (act_ref[0] == 1))
    def _():
        start(0, 0)

    nxt = jnp.minimum(s + 1, pl.num_programs(0) - 1)

    @pl.when((s + 1 < pl.num_programs(0)) & (act_ref[nxt] == 1))
    def _():
        start(nxt, 1 - slot)

    @pl.when(first_ref[s] == 1)
    def _():
        wg_s[...] = wg_ref[0].astype(BF16)
        wu_s[...] = wu_ref[0].astype(BF16)

    @pl.when(act_ref[s] == 1)
    def _():
        _row_gather_wait(x_hbm, xbuf.at[slot], sem.at[slot], tile)
        x = xbuf[slot].astype(BF16)
        o_ref[...] = (_silu(_dot(x, wg_s[...])) * _dot(x, wu_s[...])).astype(o_ref.dtype)

    @pl.when(act_ref[s] == 0)
    def _():
        o_ref[...] = jnp.zeros_like(o_ref)


def _moe_down_body(tl_ref, ex_ref, wck_ref, ock_ref, first_ref, act_ref, a_ref, wd_ref, o_ref, wd_s):
    s = pl.program_id(0)

    @pl.when(first_ref[s] == 1)
    def _():
        wd_s[...] = wd_ref[0].astype(BF16)

    @pl.when(act_ref[s] == 1)
    def _():
        o_ref[...] = _dot(a_ref[...], wd_s[...])

    @pl.when(act_ref[s] == 0)
    def _():
        o_ref[...] = jnp.zeros_like(o_ref)


def _moe_experts(x, src, w_gate, w_up, w_down, steps, n_tiles):
    tile = MOE_TILE
    d, de = w_gate.shape[1], w_gate.shape[2]
    nc = MOE_NCHUNK
    fc, dc = de // nc, d // nc
    n_steps = nc * n_tiles
    rows = n_tiles * tile
    wmap = lambda s, tl, ex, wck, ock, fi, ac, *_: (ex[s], 0, wck[s])
    omap = lambda s, tl, ex, wck, ock, fi, ac, *_: (tl[s], ock[s])
    act = pl.pallas_call(
        _moe_up_body,
        grid_spec=pltpu.PrefetchScalarGridSpec(
            num_scalar_prefetch=7, grid=(n_steps,),
            in_specs=[pl.BlockSpec(memory_space=pl.ANY),
                      pl.BlockSpec((1, d, fc), wmap), pl.BlockSpec((1, d, fc), wmap)],
            out_specs=pl.BlockSpec((tile, fc), omap),
            scratch_shapes=[pltpu.VMEM((d, fc), BF16), pltpu.VMEM((d, fc), BF16),
                            pltpu.VMEM((2, tile, d), F32), pltpu.SemaphoreType.DMA((2,))]),
        out_shape=jax.ShapeDtypeStruct((rows, de), BF16),
        compiler_params=_params(("arbitrary",)), name="moe_up",
    )(*steps, src, x, w_gate, w_up)
    y = pl.pallas_call(
        _moe_down_body,
        grid_spec=pltpu.PrefetchScalarGridSpec(
            num_scalar_prefetch=6, grid=(n_steps,),
            in_specs=[pl.BlockSpec((tile, de), lambda s, tl, *_: (tl[s], 0)), pl.BlockSpec((1, de, dc), wmap)],
            out_specs=pl.BlockSpec((tile, dc), omap),
            scratch_shapes=[pltpu.VMEM((de, dc), BF16)]),
        out_shape=jax.ShapeDtypeStruct((rows, d), F32),
        compiler_params=_params(("arbitrary",)), name="moe_down",
    )(*steps, act, w_down)
    return y


MM_TM = 1664
MM_TN = 256


def kernel(x_prompt, x_sample, state_hgrn, state_conv, state_ssm, c_prompt, c_sample, lb_params, w_mod, b_mod,
           norm1_g, norm2_g, w_in, hg_norm_g, conv_w, conv_b, dt_bias, a_log, d_skip, ssm_norm_g, w_branch_a,
           w_branch_b, w_out, w_router_group, b_router_group, w_router_expert, b_router_expert, w_gate_e, w_up_e,
           w_down_e, final_g):
    nb, seq, d = x_prompt.shape
    ns = x_sample.shape[0]
    assert w_mod.shape[0] == 1 and x_sample.shape[1] == 1
    n_p = nb * seq
    n = n_p + ns
    nheads = state_hgrn.shape[2]
    hgw = nheads * HG_DK
    n_ssm_heads = state_ssm.shape[2]
    inner = n_ssm_heads * SSM_HEADDIM
    ng = n_ssm_heads // SSM_HPG
    cdim = inner + 2 * ng * SSM_STATE
    col_q, col_f, col_v, col_og = 0, hgw, 2 * hgw, 2 * hgw + d
    col_z = 2 * hgw + 2 * d
    col_x = col_z + inner
    n_main = col_x + cdim
    col_gab = n_main + n_ssm_heads
    row = lambda a: a.reshape(1, -1)

    lb = jnp.cumsum(jax.nn.softmax(lb_params.astype(F32), axis=0), axis=0)[0]

    n_c = ns + nb
    n_c_pad = -(-n_c // 8) * 8
    c_all = jnp.concatenate([c_sample, c_prompt, jnp.zeros((n_c_pad - n_c, d), F32)], axis=0)
    mod = _mm(c_all, w_mod[0], tm=n_c_pad, tn=512, name="mod", a_fn=_silu,
              extras=[(row(b_mod[0]), pl.BlockSpec((1, 512), lambda i, j: (0, j)))],
              epilogue=lambda acc, b: acc + b)
    p_mod = dict(nb=nb, per_row=False, mod_row0=ns)
    s_mod = dict(nb=1, nt=1, tr=ns, per_row=True, mod_row0=0)

    xp2 = x_prompt.reshape(n_p, d)
    xs2 = x_sample.reshape(ns, d)
    h = _norm_mod(xp2, mod, row(norm1_g[0]), nt=seq // 256, tr=256, sc_blk=1, sh_blk=0, into=None, total_rows=n,
                  out_row0=0, name="norm1_prompt", **p_mod)
    h = _norm_mod(xs2, mod, row(norm1_g[0]), sc_blk=1, sh_blk=0, into=h, total_rows=n, out_row0=n_p,
                  name="norm1_sample", **s_mod)

    u = _mm(h, w_in[0], tm=MM_TM, tn=MM_TN, ncols=n_main, name="in_proj")
    u_dt = _mm(h, w_in[0], tm=MM_TM, tn=LANES, col0=n_main, ncols=n_ssm_heads, name="in_proj_dt")
    u_gab = _mm(h, w_in[0][:, col_gab:].astype(BF16), tm=MM_TM, tn=MM_TN, name="in_proj_gates")

    gn_h = row(hg_norm_g[0])
    o_a, hg_p = _hgrn_prompt(u, row(lb), gn_h, nb=nb, seq=seq, nheads=nheads, col_q=col_q, col_f=col_f,
                             col_v=col_v, col_og=col_og, total_rows=n, tb=512)
    f_t = u[n_p:, col_f:col_f + hgw].reshape(ns // STEP_TOKENS, STEP_TOKENS, hgw).transpose(0, 2, 1)
    o_a, hg_s = _hgrn_step(u, f_t, lb.reshape(hgw, 1), gn_h, state_hgrn[0], row0=n_p, nheads=nheads,
                           col_q=col_q, col_v=col_v, col_og=col_og, into=o_a)

    gn_s = row(ssm_norm_g[0])
    y_b, ssm_p = _ssd_prompt(u, u_dt, conv_w[0], row(conv_b[0]), row(dt_bias[0]), row(a_log[0]), row(d_skip[0]),
                             gn_s, nb=nb, seq=seq, col_z=col_z, col_x=col_x, total_rows=n, tb=256)
    xs, bm, cm, xdt, da = _ssd_prep(u, u_dt, state_conv[0].reshape(ns, (SSM_CONV - 1) * cdim), conv_w[0],
                                    row(conv_b[0]), row(dt_bias[0]), row(a_log[0]), row0=n_p, n=ns, col_x=col_x)
    cols = lambda a: a.reshape(ns // STEP_TOKENS, STEP_TOKENS, inner).transpose(0, 2, 1)
    dsk_full = jnp.repeat(d_skip[0], SSM_HEADDIM).reshape(1, inner)
    y_b, ssm_s = _ssd_step(u, xs, bm, cm, cols(xdt), cols(da), dsk_full, gn_s, state_ssm[0], row0=n_p,
                           col_z=col_z, into=y_b)
    xbc = u[:, col_x:col_x + cdim]
    conv_p = xbc[:n_p].reshape(nb, seq, cdim)[:, seq - (SSM_CONV - 1):]
    conv_s = jnp.concatenate([state_conv[0][:, 1:], xbc[n_p:, None, :]], axis=1)

    tile_spec = lambda off: pl.BlockSpec((MM_TM, MM_TN), lambda i, j: (i, j + off // MM_TN))
    t_a = _mm(o_a, w_branch_a[0], tm=MM_TM, tn=MM_TN, name="branch_a", extras=[(u_gab, tile_spec(0))],
              epilogue=lambda acc, ga: _sigmoid(ga) * acc)
    half = inner // 2
    p_b = _mm(y_b, w_branch_b[0], tm=MM_TM, tn=MM_TN, k=half, name="branch_b_lo")
    merged = _mm(y_b, w_branch_b[0], tm=MM_TM, tn=MM_TN, k=half, a_kblk=1, b_kblk=1, name="branch_b_hi",
                 extras=[(u_gab, tile_spec(d)), (t_a, tile_spec(0)), (p_b, tile_spec(0))],
                 epilogue=lambda acc, gb, ta, pb: ta + _sigmoid(gb) * (pb + acc), out_dtype=BF16)
    proj = _mm(merged, w_out[0], tm=MM_TM, tn=MM_TN, name="out_proj")

    wr = jnp.concatenate([w_router_group[0], w_router_expert[0],
                          jnp.zeros((d, LANES - N_GROUPS - N_EXPERTS), F32)], axis=1)
    wr3 = jnp.stack(_split3(wr))
    br = jnp.concatenate([b_router_group[0], b_router_expert[0],
                          jnp.zeros((LANES - N_GROUPS - N_EXPERTS,), F32)]).reshape(1, LANES)
    g2n = row(norm2_g[0])
    x1, h2, route = _resid_norm_router(xp2, proj, mod, g2n, wr3, br, nt=seq // 128, tr=128, gate_blk=2, sc_blk=4,
                                       sh_blk=3, proj_row0=0, intos=[None] * 3, total_rows=n,
                                       name="norm2_prompt", **p_mod)
    x1, h2, route = _resid_norm_router(xs2, proj, mod, g2n, wr3, br, gate_blk=2, sc_blk=4, sh_blk=3,
                                       proj_row0=n_p, intos=[x1, h2, route], total_rows=n,
                                       name="norm2_sample", **s_mod)

    e1 = route[:, 2].astype(jnp.int32)
    e2 = route[:, 3].astype(jnp.int32)
    n_tiles = -(-(2 * n + N_EXPERTS * (MOE_TILE - 1)) // MOE_TILE)
    pos, src, steps = _moe_plan(e1, e2, n_tiles)
    y_sorted = _moe_experts(h2, src, w_gate_e[0], w_up_e[0], w_down_e[0], steps, n_tiles)

    fg = row(final_g)
    y_prompt = _final(x1, y_sorted, pos, route, mod, fg, nt=seq // 128, tr=128, gate_blk=5, row0=0, out_rows=n_p,
                      name="final_prompt", **p_mod)
    y_sample = _final(x1, y_sorted, pos, route, mod, fg, gate_blk=5, row0=n_p, out_rows=ns, name="final_sample",
                      **s_mod)

    return (y_prompt.reshape(nb, seq, d), y_sample.reshape(ns, 1, d),
            hg_p[None], conv_p[None], ssm_p.reshape(1, nb, n_ssm_heads, SSM_HEADDIM, SSM_STATE),
            hg_s[None], conv_s[None], ssm_s[None])
```

```python
import functools

import numpy as np
import jax
import jax.numpy as jnp
from jax import lax
from jax.experimental import pallas as pl
from jax.experimental.pallas import tpu as pltpu

F32 = jnp.float32
BF16 = jnp.bfloat16
EPS = 1e-6

VMEM_LIMIT_BYTES = 56 * 1024 * 1024
LANES = 128

HG_DK = 128
HG_CHUNK = 128
HG_HEADS_PER_STEP = 4
SSM_HEADDIM = 64
SSM_STATE = 128
SSM_HPG = 16
SSM_CHUNK = 128
SSM_CONV = 4
N_GROUPS = 4
EXPERTS_PER_GROUP = 4
N_EXPERTS = 16
MOE_TILE = 256
MOE_NCHUNK = 2


def _params(sem):
    return pltpu.CompilerParams(dimension_semantics=sem, vmem_limit_bytes=VMEM_LIMIT_BYTES)


def _sigmoid(x):
    return 1.0 / (1.0 + jnp.exp(-x))


def _silu(x):
    return x * _sigmoid(x)


def _softplus(x):
    return jnp.maximum(x, 0.0) + jnp.log(1.0 + jnp.exp(-jnp.abs(x)))


def _split3(x):
    p1 = x.astype(BF16)
    r1 = x - p1.astype(F32)
    p2 = r1.astype(BF16)
    r2 = r1 - p2.astype(F32)
    return p1, p2, r2.astype(BF16)


def _dot(a, b):
    return jnp.dot(a, b, preferred_element_type=F32)


def _dot_nt(a, b):
    return lax.dot_general(a, b, (((1,), (1,)), ((), ())), preferred_element_type=F32)


def _dot_tn(a, b):
    return lax.dot_general(a, b, (((0,), (0,)), ((), ())), preferred_element_type=F32)


def _exact_dot(w3_bf16, x_f32):
    return _dot(w3_bf16, jnp.concatenate(_split3(x_f32), axis=0))


def _exact_dot_rhs(x_f32, w3_bf16):
    return _dot(jnp.concatenate(_split3(x_f32), axis=-1), w3_bf16)


def _thrice(w, axis):
    return jnp.concatenate([w, w, w], axis=axis)


def _pcall(body, *, grid, in_specs, args, outs, sem, name, scratch=(), num_prefetch=0):
    n_in = len(args) - num_prefetch
    into = [(i, o[3]) for i, o in enumerate(outs) if o[3] is not None]
    in_specs = list(in_specs) + [pl.BlockSpec(memory_space=pl.ANY)] * len(into)
    all_args = list(args) + [buf for _, buf in into]
    aliases = {len(args) + j: i for j, (i, _) in enumerate(into)}

    def wrapped(*refs):
        keep = refs[:num_prefetch + n_in] + refs[num_prefetch + n_in + len(into):]
        return body(*keep)

    grid_spec = pltpu.PrefetchScalarGridSpec(
        num_scalar_prefetch=num_prefetch, grid=grid, in_specs=in_specs,
        out_specs=[o[2] for o in outs], scratch_shapes=list(scratch))
    res = pl.pallas_call(
        wrapped, grid_spec=grid_spec,
        out_shape=[jax.ShapeDtypeStruct(o[0], o[1]) for o in outs],
        input_output_aliases=aliases, compiler_params=_params(sem), name=name,
    )(*all_args)
    return res


def _mm_body(a_ref, b_ref, *rest, n_extra, a_fn, epilogue):
    extras = rest[:n_extra]
    o_ref = rest[n_extra]
    a = a_ref[...]
    if a_fn is not None:
        a = a_fn(a)
    acc = _dot(a.astype(BF16), b_ref[...].astype(BF16))
    if epilogue is not None:
        acc = epilogue(acc, *[e[...] for e in extras])
    o_ref[...] = acc.astype(o_ref.dtype)


def _mm(a, b, *, tm, tn, name, k=None, a_kblk=0, b_kblk=0, col0=0, ncols=None, extras=(), a_fn=None,
        epilogue=None, out_dtype=F32):
    m = a.shape[0]
    k = a.shape[1] if k is None else k
    ncols = b.shape[1] if ncols is None else ncols
    assert m % tm == 0 and ncols % tn == 0 and col0 % tn == 0
    cb = col0 // tn
    in_specs = [pl.BlockSpec((tm, k), lambda i, j: (i, a_kblk)),
                pl.BlockSpec((k, tn), lambda i, j: (b_kblk, j + cb))]
    in_specs += [s for _, s in extras]
    return pl.pallas_call(
        functools.partial(_mm_body, n_extra=len(extras), a_fn=a_fn, epilogue=epilogue),
        grid=(m // tm, ncols // tn),
        in_specs=in_specs,
        out_specs=pl.BlockSpec((tm, tn), lambda i, j: (i, j)),
        out_shape=jax.ShapeDtypeStruct((m, ncols), out_dtype),
        compiler_params=_params(("parallel", "arbitrary")),
        name=name,
    )(a, b, *[x for x, _ in extras])


def _rms(x, g):
    return x * lax.rsqrt(jnp.mean(x * x, axis=-1, keepdims=True) + EPS) * g


def _mod_view(mod, per_row):
    return mod if per_row else mod.reshape(mod.shape[0], 1, mod.shape[1])


def _mod_spec(per_row, tr, d, col_blk, nt, row0):
    if per_row:
        rb = row0 // tr
        return pl.BlockSpec((tr, d), lambda b, t: (rb + t, col_blk))
    return pl.BlockSpec((1, 1, d), lambda b, t: (row0 + b, 0, col_blk))


def _rows_spec(tr, d, nt, row0):
    rb = row0 // tr
    return pl.BlockSpec((tr, d), lambda b, t: (rb + b * nt + t, 0))


def _norm_mod_body(x_ref, sc_ref, sh_ref, g_ref, o_ref):
    d = x_ref.shape[-1]
    sc = sc_ref[...].reshape(-1, d)
    sh = sh_ref[...].reshape(-1, d)
    o_ref[...] = (_rms(x_ref[...], g_ref[...]) * (1.0 + sc) + sh).astype(o_ref.dtype)


def _norm_mod(x, mod, g, *, nb, nt, tr, per_row, mod_row0, sc_blk, sh_blk, into, total_rows, out_row0, name):
    d = x.shape[-1]
    mv = _mod_view(mod, per_row)
    return _pcall(
        _norm_mod_body, grid=(nb, nt),
        in_specs=[_rows_spec(tr, d, nt, 0),
                  _mod_spec(per_row, tr, d, sc_blk, nt, mod_row0),
                  _mod_spec(per_row, tr, d, sh_blk, nt, mod_row0),
                  pl.BlockSpec((1, d), lambda b, t: (0, 0))],
        args=[x, mv, mv, g],
        outs=[((total_rows, d), BF16, _rows_spec(tr, d, nt, out_row0), into)],
        sem=("parallel", "arbitrary"), name=name)[0]


def _router_math(lg):
    lane = lax.broadcasted_iota(jnp.int32, lg.shape, 1)
    neg = jnp.float32(-jnp.inf)
    gm = jnp.where(lane < N_GROUPS, lg, neg)
    gmax = jnp.max(gm, axis=-1, keepdims=True)
    gidx = jnp.min(jnp.where(gm == gmax, lane, LANES), axis=-1, keepdims=True)
    gsum = jnp.sum(jnp.where(lane < N_GROUPS, jnp.exp(gm - gmax), 0.0), axis=-1, keepdims=True)
    p_group = 1.0 / gsum
    lo = N_GROUPS + EXPERTS_PER_GROUP * gidx
    em = jnp.where((lane >= lo) & (lane < lo + EXPERTS_PER_GROUP), lg, neg)
    m1 = jnp.max(em, axis=-1, keepdims=True)
    i1 = jnp.min(jnp.where(em == m1, lane, LANES), axis=-1, keepdims=True)
    em2 = jnp.where(lane == i1, neg, em)
    m2 = jnp.max(em2, axis=-1, keepdims=True)
    i2 = jnp.min(jnp.where(em2 == m2, lane, LANES), axis=-1, keepdims=True)
    r = jnp.exp(m2 - m1)
    w1 = p_group / (1.0 + r)
    w2 = p_group * r / (1.0 + r)
    e1 = (i1 - N_GROUPS).astype(F32)
    e2 = (i2 - N_GROUPS).astype(F32)
    out = jnp.where(lane == 0, w1, 0.0)
    out = jnp.where(lane == 1, w2, out)
    out = jnp.where(lane == 2, e1, out)
    out = jnp.where(lane == 3, e2, out)
    return out


def _resid_norm_router_body(x_ref, p_ref, gt_ref, sc_ref, sh_ref, g_ref, wr_ref, br_ref, x1_ref, h_ref, rt_ref):
    d = x_ref.shape[-1]
    gt = gt_ref[...].reshape(-1, d)
    sc = sc_ref[...].reshape(-1, d)
    sh = sh_ref[...].reshape(-1, d)
    x1 = x_ref[...] + gt * p_ref[...]
    x1_ref[...] = x1
    h = _rms(x1, g_ref[...]) * (1.0 + sc) + sh
    h_ref[...] = h.astype(h_ref.dtype)
    h1, h2, h3 = _split3(h)
    w1, w2, w3 = wr_ref[0], wr_ref[1], wr_ref[2]
    lg = (_dot(h1, w1) + (_dot(h1, w2) + _dot(h2, w1))
          + (_dot(h1, w3) + _dot(h2, w2) + _dot(h3, w1)))
    rt_ref[...] = _router_math(lg + br_ref[...])


def _resid_norm_router(x, proj, mod, g, wr3, br, *, nb, nt, tr, per_row, mod_row0, gate_blk, sc_blk, sh_blk,
                       proj_row0, intos, total_rows, name):
    d = x.shape[-1]
    mv = _mod_view(mod, per_row)
    ms = lambda blk: _mod_spec(per_row, tr, d, blk, nt, mod_row0)
    orow = _rows_spec(tr, d, nt, proj_row0)
    return _pcall(
        _resid_norm_router_body, grid=(nb, nt),
        in_specs=[_rows_spec(tr, d, nt, 0), orow, ms(gate_blk), ms(sc_blk), ms(sh_blk),
                  pl.BlockSpec((1, d), lambda b, t: (0, 0)),
                  pl.BlockSpec((3, d, LANES), lambda b, t: (0, 0, 0)),
                  pl.BlockSpec((1, LANES), lambda b, t: (0, 0))],
        args=[x, proj, mv, mv, mv, g, wr3, br],
        outs=[((total_rows, d), F32, orow, intos[0]),
              ((total_rows, d), F32, orow, intos[1]),
              ((total_rows, LANES), F32, _rows_spec(tr, LANES, nt, proj_row0), intos[2])],
        sem=("parallel", "arbitrary"), name=name)


def _row_gather_start(src_hbm, idx_ref, idx0, dst, sem, n, unrolled=False):
    def body(r, carry):
        pltpu.make_async_copy(src_hbm.at[pl.ds(idx_ref[idx0 + r], 1), :], dst.at[pl.ds(r, 1), :], sem).start()
        return carry

    if unrolled:
        for r in range(n):
            body(r, 0)
    else:
        lax.fori_loop(0, n, body, 0, unroll=8)


def _row_gather_wait(src_hbm, dst, sem, n):
    def body(r, carry):
        pltpu.make_async_copy(src_hbm.at[pl.ds(0, 1), :], dst.at[pl.ds(r, 1), :], sem).wait()
        return carry

    lax.fori_loop(0, n, body, 0, unroll=8)


def _final_body(pos_ref, x1_ref, rt_ref, gt_ref, g_ref, y_hbm, o_ref, ybuf, sem, *, tr, tok0, n_tok):
    d = x1_ref.shape[-1]
    i = pl.program_id(0)
    slot = lax.rem(i, 2)

    def start(step, sl):
        for k in range(2):
            _row_gather_start(y_hbm, pos_ref, k * n_tok + tok0 + step * tr, ybuf.at[sl, k], sem.at[sl], tr)

    @pl.when(i == 0)
    def _():
        start(0, 0)

    @pl.when(i + 1 < pl.num_programs(0))
    def _():
        start(i + 1, 1 - slot)

    for k in range(2):
        _row_gather_wait(y_hbm, ybuf.at[slot, k], sem.at[slot], tr)
    gt = gt_ref[...].reshape(-1, d)
    rt = rt_ref[...]
    moe = rt[:, 0:1] * ybuf[slot, 0] + rt[:, 1:2] * ybuf[slot, 1]
    o_ref[...] = _rms(x1_ref[...] + gt * moe, g_ref[...])


def _final(x1, y_sorted, pos, route, mod, g, *, nb, nt, tr, per_row, mod_row0, gate_blk, row0, out_rows, name):
    d = x1.shape[-1]
    n_tok = x1.shape[0]
    rb = row0 // tr
    if per_row:
        mspec = pl.BlockSpec((tr, d), lambda i, pos_: (mod_row0 // tr + i, gate_blk))
    else:
        mspec = pl.BlockSpec((1, 1, d), lambda i, pos_: (mod_row0 + i // nt, 0, gate_blk))
    return _pcall(
        functools.partial(_final_body, tr=tr, tok0=row0, n_tok=n_tok), grid=(nb * nt,),
        in_specs=[pl.BlockSpec((tr, d), lambda i, pos_: (rb + i, 0)),
                  pl.BlockSpec((tr, LANES), lambda i, pos_: (rb + i, 0)),
                  mspec,
                  pl.BlockSpec((1, d), lambda i, pos_: (0, 0)),
                  pl.BlockSpec(memory_space=pl.ANY)],
        args=[pos, x1, route, _mod_view(mod, per_row), g, y_sorted],
        outs=[((out_rows, d), F32, pl.BlockSpec((tr, d), lambda i, pos_: (i, 0)), None)],
        sem=("arbitrary",), name=name, num_prefetch=1,
        scratch=[pltpu.VMEM((2, 2, tr, d), F32), pltpu.SemaphoreType.DMA((2,))])[0]


HG_MXU_LEVELS = 3


def _hgrn_tables(c):
    nlev = int(np.log2(c))
    t = np.arange(c)[:, None]
    u = np.arange(c)[None, :]
    mats = [u <= t]
    masks = []
    for l in range(nlev):
        m = 1 << l
        p = (t // (2 * m)) * (2 * m) + m - 1
        upper = ((t >> l) & 1) == 1
        if l < HG_MXU_LEVELS:
            mats.append(np.where(upper, (u > p) & (u <= t), (u > t) & (u <= p)))
        same = (t >> (l + 1)) == (u >> (l + 1))
        masks.append(same & upper & (((u >> l) & 1) == 0))
    masks.append(t == u)
    w = np.concatenate([m.astype(np.float32) for m in mats], axis=0)
    return jnp.asarray(np.tile(w, (1, 3)), BF16), jnp.asarray(np.stack(masks).astype(np.float32))


def _hgrn_prompt_body(q_ref, f_ref, v_ref, og_ref, lb_ref, gn_ref, w_ref, mk_ref, o_ref, s_ref, st_scr, *,
                      nchunks, nlev):
    c = HG_CHUNK
    hp = st_scr.shape[0]
    t = pl.program_id(2)

    @pl.when(t == 0)
    def _():
        st_scr[...] = jnp.zeros_like(st_scr)

    lb = lb_ref[...]
    gn = gn_ref[...]
    trow = lax.broadcasted_iota(jnp.int32, (c, HG_DK), 0)
    upper = [((trow >> l) & 1) == 1 for l in range(HG_MXU_LEVELS, nlev)]

    def chunk(ci, carry):
        r0 = pl.multiple_of(ci * c, c)
        rows = pl.ds(r0, c)
        for i in range(hp):
            ln = slice(i * HG_DK, (i + 1) * HG_DK)
            f = lb[:, ln] + (1.0 - lb[:, ln]) * _sigmoid(f_ref[rows, ln])
            g = _exact_dot(w_ref[...], jnp.log2(f))
            kk = 1.0 - f
            q = _silu(q_ref[rows, ln])
            v = v_ref[rows, ln].astype(BF16)
            b = g[0:c]
            b_end = b[c - 1:c, :]
            st = st_scr[i]
            o = _dot_nt((q * jnp.exp2(b)).astype(BF16), st.astype(BF16))
            scores = mk_ref[nlev] * _dot_nt(q.astype(BF16), kk.astype(BF16))
            for l in range(nlev):
                if l < HG_MXU_LEVELS:
                    x = jnp.exp2(g[(1 + l) * c:(2 + l) * c])
                else:
                    m = 1 << l
                    b3 = b.reshape(c // (2 * m), 2 * m, HG_DK)
                    d = (b3 - b3[:, m - 1:m, :]).reshape(c, HG_DK)
                    x = jnp.exp2(jnp.where(upper[l - HG_MXU_LEVELS], d, -d))
                scores = scores + mk_ref[l] * _dot_nt((q * x).astype(BF16), (kk * x).astype(BF16))
            o = o + _dot(scores.astype(BF16), v)
            kw = (kk * jnp.exp2(b_end - b)).astype(BF16)
            st_scr[i] = st * jnp.exp2(b_end) + _dot_tn(v, kw)
            o = _rms(o, gn[:, ln]) * _silu(og_ref[rows, ln])
            o_ref[rows, ln] = o.astype(o_ref.dtype)
        return carry

    lax.fori_loop(0, nchunks, chunk, 0)

    @pl.when(t == pl.num_programs(2) - 1)
    def _():
        for i in range(hp):
            s_ref[0, i] = st_scr[i].T


def _hgrn_prompt(u, lb, gn, *, nb, seq, nheads, col_q, col_f, col_v, col_og, total_rows, tb):
    c = HG_CHUNK
    nt = seq // tb
    nlev = int(np.log2(c))
    w, mk = _hgrn_tables(c)
    hp = HG_HEADS_PER_STEP
    wd = hp * HG_DK
    cs = lambda col: pl.BlockSpec((tb, wd), lambda b, h, t: (b * nt + t, col // wd + h))
    hv = pl.BlockSpec((1, wd), lambda b, h, t: (0, h))
    return _pcall(
        functools.partial(_hgrn_prompt_body, nchunks=tb // c, nlev=nlev),
        grid=(nb, nheads // hp, nt),
        in_specs=[cs(col_q), cs(col_f), cs(col_v), cs(col_og), hv, hv,
                  pl.BlockSpec(w.shape, lambda b, h, t: (0, 0)),
                  pl.BlockSpec(mk.shape, lambda b, h, t: (0, 0, 0))],
        args=[u, u, u, u, lb, gn, w, mk],
        outs=[((total_rows, nheads * HG_DK), BF16, pl.BlockSpec((tb, wd), lambda b, h, t: (b * nt + t, h)), None),
              ((nb, nheads, HG_DK, HG_DK), F32, pl.BlockSpec((1, hp, HG_DK, HG_DK), lambda b, h, t: (b, h, 0, 0)), None)],
        sem=("parallel", "parallel", "arbitrary"), name="hgrn_prompt",
        scratch=[pltpu.VMEM((hp, HG_DK, HG_DK), F32)])


def _ssm_expand_tables(n_heads):
    ng = n_heads // SSM_HPG
    w2, w1 = SSM_HPG * LANES, SSM_HPG * SSM_HEADDIM
    e = np.zeros((ng, n_heads, w2 + w1), np.float32)
    for g in range(ng):
        for i in range(SSM_HPG):
            e[g, g * SSM_HPG + i, i * LANES:(i + 1) * LANES] = 1.0
            e[g, g * SSM_HPG + i, w2 + i * SSM_HEADDIM:w2 + (i + 1) * SSM_HEADDIM] = 1.0
    return jnp.asarray(np.tile(e, (1, 3, 1)), BF16)


def _conv_silu(buf_ref, w, bias, r0, n):
    b = buf_ref[pl.ds(r0, n + 8), :]
    acc = b * w[0:1, :]
    for j in range(1, SSM_CONV):
        acc = pltpu.roll(acc, 1, axis=0) + b * w[j:j + 1, :]
    return _silu(acc[8:, :] + bias)


def _ssd_prompt_body(x_ref, b_ref, c_ref, z_ref, dt_ref, wx_ref, wb_ref, wc_ref, bx_ref, bb_ref, bc_ref,
                     dtb_ref, alog_ref, dsk_ref, e_ref, gn_ref, y_ref, s_ref,
                     ht_scr, cbx, cbb, cbc, *, nchunks):
    L = SSM_CHUNK
    tb = nchunks * L
    w2 = SSM_HPG * LANES
    t = pl.program_id(2)

    @pl.when(t == 0)
    def _():
        ht_scr[...] = jnp.zeros_like(ht_scr)
        cbx[0:8, :] = jnp.zeros((8, cbx.shape[1]), F32)
        cbb[0:8, :] = jnp.zeros((8, cbb.shape[1]), F32)
        cbc[0:8, :] = jnp.zeros((8, cbc.shape[1]), F32)

    cbx[8:8 + tb, :] = x_ref[...]
    cbb[8:8 + tb, :] = b_ref[...]
    cbc[8:8 + tb, :] = c_ref[...]

    e_all = e_ref[0]
    e1 = e_all[:, w2:]
    neg_a = -jnp.exp(alog_ref[...])
    dsk1 = _exact_dot_rhs(dsk_ref[...], e1)
    row = lax.broadcasted_iota(jnp.int32, (L, L), 0)
    col = lax.broadcasted_iota(jnp.int32, (L, L), 1)
    causal = col <= row
    tril = _thrice(jnp.where(causal, 1.0, 0.0).astype(BF16), 1)
    lane = lax.broadcasted_iota(jnp.int32, (L, LANES), 1)
    lo_half = lane < SSM_HEADDIM

    for ci in range(nchunks):
        r0 = ci * L
        xs = _conv_silu(cbx, wx_ref[...], bx_ref[...], r0, L)
        bm = _conv_silu(cbb, wb_ref[...], bb_ref[...], r0, L).astype(BF16)
        cm = _conv_silu(cbc, wc_ref[...], bc_ref[...], r0, L).astype(BF16)
        dt = _softplus(dt_ref[r0:r0 + L, :] + dtb_ref[...])
        cum = _exact_dot(tril, dt * neg_a)
        cum_e = _exact_dot_rhs(cum, e_all)
        cum1 = cum_e[:, w2:]
        dt1 = _exact_dot_rhs(dt, e1)
        xdt = xs * dt1
        cum_end = cum1[L - 1:L, :]
        ht = ht_scr[...]
        y = _dot(cm, ht.astype(BF16)) * jnp.exp(cum1)
        cb = _dot_nt(cm, bm)
        parts = []
        for j in range(SSM_HPG // 2):
            xp = xdt[:, j * LANES:(j + 1) * LANES]
            acc = None
            for half in range(2):
                i = 2 * j + half
                colb = cum_e[:, i * LANES:(i + 1) * LANES]
                seg = colb - colb.T
                m = (cb * jnp.exp(jnp.where(causal, seg, -jnp.inf))).astype(BF16)
                xh = jnp.where(lo_half if half == 0 else ~lo_half, xp, 0.0).astype(BF16)
                d = _dot(m, xh)
                acc = d if acc is None else acc + d
            parts.append(acc)
        y = y + jnp.concatenate(parts, axis=-1) + dsk1 * xs
        y = y * _silu(z_ref[r0:r0 + L, :])
        y_ref[r0:r0 + L, :] = _rms(y, gn_ref[...]).astype(y_ref.dtype)
        xw = (xdt * jnp.exp(cum_end - cum1)).astype(BF16)
        ht_scr[...] = ht * jnp.exp(cum_end) + _dot_tn(bm, xw)

    cbx[0:8, :] = cbx[tb:tb + 8, :]
    cbb[0:8, :] = cbb[tb:tb + 8, :]
    cbc[0:8, :] = cbc[tb:tb + 8, :]

    @pl.when(t == pl.num_programs(2) - 1)
    def _():
        s_ref[0, 0] = ht_scr[...].T


def _ssd_prompt(u, u_dt, conv_w, conv_b, dt_bias, a_log, d_skip, gn, *, nb, seq, col_z, col_x, total_rows, tb):
    n_heads = u_dt.shape[1]
    ng = n_heads // SSM_HPG
    w1 = SSM_HPG * SSM_HEADDIM
    inner = ng * w1
    nt = seq // tb
    e = _ssm_expand_tables(n_heads)
    xb, bb, cb_, zb = col_x // w1, (col_x + inner) // LANES, (col_x + inner + ng * SSM_STATE) // LANES, col_z // w1
    wide = lambda blk0: pl.BlockSpec((tb, w1), lambda b, g, t: (b * nt + t, blk0 + g))
    thin = lambda blk0: pl.BlockSpec((tb, LANES), lambda b, g, t: (b * nt + t, blk0 + g))
    cw = lambda rows, width, blk0: pl.BlockSpec((rows, width), lambda b, g, t: (0, blk0 + g))
    full = lambda a: pl.BlockSpec(a.shape, lambda b, g, t: (0,) * a.ndim)
    return _pcall(
        functools.partial(_ssd_prompt_body, nchunks=tb // SSM_CHUNK),
        grid=(nb, ng, nt),
        in_specs=[wide(xb), thin(bb), thin(cb_), wide(zb),
                  pl.BlockSpec((tb, n_heads), lambda b, g, t: (b * nt + t, 0)),
                  cw(SSM_CONV, w1, 0), cw(SSM_CONV, LANES, inner // LANES),
                  cw(SSM_CONV, LANES, (inner + ng * SSM_STATE) // LANES),
                  cw(1, w1, 0), cw(1, LANES, inner // LANES), cw(1, LANES, (inner + ng * SSM_STATE) // LANES),
                  full(dt_bias), full(a_log), full(d_skip),
                  pl.BlockSpec((1,) + e.shape[1:], lambda b, g, t: (g, 0, 0)),
                  cw(1, w1, 0)],
        args=[u, u, u, u, u_dt, conv_w, conv_w, conv_w, conv_b, conv_b, conv_b, dt_bias, a_log, d_skip, e, gn],
        outs=[((total_rows, inner), BF16, pl.BlockSpec((tb, w1), lambda b, g, t: (b * nt + t, g)), None),
              ((nb, ng, w1, SSM_STATE), F32, pl.BlockSpec((1, 1, w1, SSM_STATE), lambda b, g, t: (b, g, 0, 0)), None)],
        sem=("parallel", "parallel", "arbitrary"), name="ssd_prompt",
        scratch=[pltpu.VMEM((SSM_STATE, w1), F32), pltpu.VMEM((tb + 8, w1), F32),
                 pltpu.VMEM((tb + 8, LANES), F32), pltpu.VMEM((tb + 8, LANES), F32)])


STEP_TOKENS = 16
STEP_HEADS = 4


def _hgrn_step_body(q_ref, v_ref, og_ref, ft_ref, lbc_ref, gn_ref, s_ref, o_ref, so_ref):
    lbc = lbc_ref[...]
    f = lbc + (1.0 - lbc) * _sigmoid(ft_ref[0])
    kk = 1.0 - f
    q = _silu(q_ref[...])
    v = v_ref[...]
    gn = gn_ref[...]
    og = og_ref[...]
    for i in range(STEP_HEADS):
        ln = slice(i * HG_DK, (i + 1) * HG_DK)
        rows = []
        for j in range(STEP_TOKENS):
            sn = f[ln, j:j + 1] * s_ref[j, i] + kk[ln, j:j + 1] * v[j:j + 1, ln]
            so_ref[j, i] = sn
            qj = jnp.broadcast_to(q[j:j + 1, ln], (16, HG_DK)).astype(BF16)
            rows.append(_dot(qj, sn.astype(BF16))[0:1, :])
        o = jnp.concatenate(rows, axis=0)
        o_ref[:, ln] = (_rms(o, gn[:, ln]) * _silu(og[:, ln])).astype(o_ref.dtype)


def _hgrn_step(u, f_t, lb_col, gn, state, *, row0, nheads, col_q, col_v, col_og, into):
    n = state.shape[0]
    tb = STEP_TOKENS
    hp = STEP_HEADS
    wd = hp * HG_DK
    rb = row0 // tb
    cs = lambda col: pl.BlockSpec((tb, wd), lambda b, h: (rb + b, col // wd + h))
    sspec = pl.BlockSpec((tb, hp, HG_DK, HG_DK), lambda b, h: (b, h, 0, 0))
    return _pcall(
        _hgrn_step_body, grid=(n // tb, nheads // hp),
        in_specs=[cs(col_q), cs(col_v), cs(col_og),
                  pl.BlockSpec((1, wd, tb), lambda b, h: (b, h, 0)),
                  pl.BlockSpec((wd, 1), lambda b, h: (h, 0)),
                  pl.BlockSpec((1, wd), lambda b, h: (0, h)),
                  sspec],
        args=[u, u, u, f_t, lb_col, gn, state],
        outs=[(into.shape, into.dtype, pl.BlockSpec((tb, wd), lambda b, h: (rb + b, h)), into),
              (state.shape, F32, sspec, None)],
        sem=("parallel", "arbitrary"), name="hgrn_step")


def _ssd_prep_body(x_ref, b_ref, c_ref, sx0, sx1, sx2, sb0, sb1, sb2, sc0, sc1, sc2, dt_ref,
                   wx_ref, wb_ref, wc_ref, bx_ref, bb_ref, bc_ref, dtb_ref, alog_ref, e_ref,
                   xs_ref, bm_ref, cm_ref, xdt_ref, da_ref):
    def conv(taps, cur, w_ref, bias_ref):
        w = w_ref[...]
        acc = bias_ref[...] + cur[...] * w[SSM_CONV - 1:SSM_CONV, :]
        for j, tap in enumerate(taps):
            acc = acc + tap[...] * w[j:j + 1, :]
        return _silu(acc)

    xs = conv((sx0, sx1, sx2), x_ref, wx_ref, bx_ref)
    xs_ref[...] = xs
    bm_ref[...] = conv((sb0, sb1, sb2), b_ref, wb_ref, bb_ref)
    cm_ref[...] = conv((sc0, sc1, sc2), c_ref, wc_ref, bc_ref)
    e1 = e_ref[0][:, SSM_HPG * LANES:]
    dt = _softplus(dt_ref[...] + dtb_ref[...])
    xdt_ref[...] = xs * _exact_dot_rhs(dt, e1)
    da_ref[...] = _exact_dot_rhs(jnp.exp(dt * -jnp.exp(alog_ref[...])), e1)


def _ssd_prep(u, u_dt, conv_state2d, conv_w, conv_b, dt_bias, a_log, *, row0, n, col_x):
    n_heads = u_dt.shape[1]
    ng = n_heads // SSM_HPG
    w1 = SSM_HPG * SSM_HEADDIM
    inner = ng * w1
    cdim = inner + 2 * ng * SSM_STATE
    e = _ssm_expand_tables(n_heads)
    rb = row0 // n
    xoff, boff, coff = 0, inner, inner + ng * SSM_STATE
    cur = lambda width, off: pl.BlockSpec((n, width), lambda g: (rb, (col_x + off) // width + g))
    tap = lambda width, off, j: pl.BlockSpec((n, width), lambda g: (0, (j * cdim + off) // width + g))
    cw = lambda rows, width, off: pl.BlockSpec((rows, width), lambda g: (0, off // width + g))
    full = lambda a: pl.BlockSpec(a.shape, lambda g: (0,) * a.ndim)
    taps = [tap(w1, xoff, j) for j in range(3)] + [tap(LANES, boff, j) for j in range(3)] + \
           [tap(LANES, coff, j) for j in range(3)]
    o_wide = pl.BlockSpec((n, w1), lambda g: (0, g))
    o_thin = pl.BlockSpec((n, LANES), lambda g: (0, g))
    return _pcall(
        _ssd_prep_body, grid=(ng,),
        in_specs=[cur(w1, xoff), cur(LANES, boff), cur(LANES, coff)] + taps +
                 [pl.BlockSpec((n, n_heads), lambda g: (rb, 0)),
                  cw(SSM_CONV, w1, xoff), cw(SSM_CONV, LANES, boff), cw(SSM_CONV, LANES, coff),
                  cw(1, w1, xoff), cw(1, LANES, boff), cw(1, LANES, coff),
                  full(dt_bias), full(a_log),
                  pl.BlockSpec((1,) + e.shape[1:], lambda g: (g, 0, 0))],
        args=[u, u, u] + [conv_state2d] * 9 + [u_dt, conv_w, conv_w, conv_w, conv_b, conv_b, conv_b,
                                               dt_bias, a_log, e],
        outs=[((n, inner), F32, o_wide, None), ((n, ng * SSM_STATE), F32, o_thin, None),
              ((n, ng * SSM_STATE), F32, o_thin, None), ((n, inner), F32, o_wide, None),
              ((n, inner), F32, o_wide, None)],
        sem=("arbitrary",), name="ssd_prep")


def _ssd_step_body(xs_ref, bm_ref, cm_ref, z_ref, xdt_ref, da_ref, dsk_ref, gn_ref, s_ref, y_ref, so_ref):
    w1 = SSM_HPG * SSM_HEADDIM
    bm = bm_ref[...]
    cm = cm_ref[...]
    xdt = xdt_ref[0]
    da = da_ref[0]
    rows = []
    for j in range(STEP_TOKENS):
        h = s_ref[j].reshape(w1, SSM_STATE)
        hn = da[:, j:j + 1] * h + xdt[:, j:j + 1] * bm[j:j + 1, :]
        so_ref[j] = hn.reshape(SSM_HPG, SSM_HEADDIM, SSM_STATE)
        cj = jnp.broadcast_to(cm[j:j + 1, :], (16, SSM_STATE)).astype(BF16)
        rows.append(_dot_nt(cj, hn.astype(BF16))[0:1, :])
    y = jnp.concatenate(rows, axis=0) + dsk_ref[...] * xs_ref[...]
    y = y * _silu(z_ref[...])
    y_ref[...] = _rms(y, gn_ref[...]).astype(y_ref.dtype)


def _ssd_step(u, xs, bm, cm, xdt_t, da_t, dsk_full, gn, state, *, row0, col_z, into):
    n, n_heads = state.shape[0], state.shape[1]
    ng = n_heads // SSM_HPG
    w1 = SSM_HPG * SSM_HEADDIM
    tb = STEP_TOKENS
    rb = row0 // tb
    sspec = pl.BlockSpec((tb, SSM_HPG, SSM_HEADDIM, SSM_STATE), lambda b, g: (b, g, 0, 0))
    colspec = pl.BlockSpec((1, w1, tb), lambda b, g: (b, g, 0))
    return _pcall(
        _ssd_step_body, grid=(n // tb, ng),
        in_specs=[pl.BlockSpec((tb, w1), lambda b, g: (b, g)),
                  pl.BlockSpec((tb, SSM_STATE), lambda b, g: (b, g)),
                  pl.BlockSpec((tb, SSM_STATE), lambda b, g: (b, g)),
                  pl.BlockSpec((tb, w1), lambda b, g: (rb + b, col_z // w1 + g)),
                  colspec, colspec,
                  pl.BlockSpec((1, w1), lambda b, g: (0, g)),
                  pl.BlockSpec((1, w1), lambda b, g: (0, g)),
                  sspec],
        args=[xs, bm, cm, u, xdt_t, da_t, dsk_full, gn, state],
        outs=[(into.shape, into.dtype, pl.BlockSpec((tb, w1), lambda b, g: (rb + b, g)), into),
              (state.shape, F32, sspec, None)],
        sem=("parallel", "arbitrary"), name="ssd_step")


def _moe_plan(e1, e2, n_tiles):
    n = e1.shape[0]
    tile = MOE_TILE
    eid = jnp.concatenate([e1, e2]).astype(jnp.int32)
    onehot = (eid[:, None] == jnp.arange(N_EXPERTS, dtype=jnp.int32)[None, :]).astype(jnp.int32)
    rank = jnp.sum((jnp.cumsum(onehot, axis=0) - onehot) * onehot, axis=1)
    counts = jnp.sum(onehot, axis=0)
    ptiles = (counts + tile - 1) // tile
    tstart = jnp.cumsum(ptiles) - ptiles
    pos = (tstart * tile)[eid] + rank
    tok = jnp.concatenate([jnp.arange(n, dtype=jnp.int32)] * 2)
    src = jnp.zeros((n_tiles * tile,), jnp.int32).at[pos].set(tok)
    n_used = jnp.sum(ptiles)
    nc = MOE_NCHUNK
    s = jnp.arange(nc * n_tiles, dtype=jnp.int32)
    send = jnp.cumsum(ptiles) * nc
    s_eff = jnp.minimum(s, nc * n_used - 1)
    ex = jnp.searchsorted(send, s_eff, side="right").astype(jnp.int32)
    local = s_eff - (send - ptiles * nc)[ex]
    pt = jnp.maximum(ptiles[ex], 1)
    chunk = local // pt
    tl = tstart[ex] + local % pt
    active = (s < nc * n_used).astype(jnp.int32)
    idle = s - nc * n_used
    tl = jnp.where(active == 1, tl, n_used + idle // nc)
    ock = jnp.where(active == 1, chunk, idle % nc)
    wid = ex * nc + chunk
    first = jnp.concatenate([jnp.ones((1,), jnp.int32), (wid[1:] != wid[:-1]).astype(jnp.int32)])
    i32 = lambda a: a.astype(jnp.int32)
    return i32(pos), src, (i32(tl), i32(ex), i32(chunk), i32(ock), first, active)


def _moe_up_body(tl_ref, ex_ref, wck_ref, ock_ref, first_ref, act_ref, src_ref, x_hbm, wg_ref, wu_ref, o_ref,
                 wg_s, wu_s, xbuf, sem):
    s = pl.program_id(0)
    last = pl.num_programs(0) - 1
    slot = lax.rem(s, 2)
    tile = xbuf.shape[1]
    nxt = jnp.minimum(s + 1, last)

    def start(step, sl, unrolled=False):
        _row_gather_start(x_hbm, src_ref, tl_ref[step] * tile, xbuf.at[sl], sem.at[sl], tile, unrolled)

    @pl.when(s == 0)
    def _():
        start(0, 0)

    @pl.when(first_ref[s] == 1)
    def _():
        wg_s[...] = wg_ref[0].astype(BF16)
        wu_s[...] = wu_ref[0].astype(BF16)

    _row_gather_wait(x_hbm, xbuf.at[slot], sem.at[slot], tile)

    @pl.when(act_ref[s] == 1)
    def _():
        x = xbuf[slot].astype(BF16)
        start(nxt, 1 - slot, unrolled=True)
        o_ref[...] = (_silu(_dot(x, wg_s[...])) * _dot(x, wu_s[...])).astype(o_ref.dtype)

    @pl.when(act_ref[s] == 0)
    def _():
        start(nxt, 1 - slot)
        o_ref[...] = jnp.zeros_like(o_ref)

    @pl.when(s == last)
    def _():
        _row_gather_wait(x_hbm, xbuf.at[1 - slot], sem.at[1 - slot], tile)


def _moe_down_body(tl_ref, ex_ref, wck_ref, ock_ref, first_ref, act_ref, a_ref, wd_ref, o_ref, wd_s):
    s = pl.program_id(0)

    @pl.when(first_ref[s] == 1)
    def _():
        wd_s[...] = wd_ref[0].astype(BF16)

    @pl.when(act_ref[s] == 1)
    def _():
        o_ref[...] = _dot(a_ref[...], wd_s[...])

    @pl.when(act_ref[s] == 0)
    def _():
        o_ref[...] = jnp.zeros_like(o_ref)


def _moe_experts(x, src, w_gate, w_up, w_down, steps, n_tiles):
    tile = MOE_TILE
    d, de = w_gate.shape[1], w_gate.shape[2]
    nc = MOE_NCHUNK
    fc, dc = de // nc, d // nc
    n_steps = nc * n_tiles
    rows = n_tiles * tile
    wmap = lambda s, tl, ex, wck, ock, fi, ac, *_: (ex[s], 0, wck[s])
    omap = lambda s, tl, ex, wck, ock, fi, ac, *_: (tl[s], ock[s])
    act = pl.pallas_call(
        _moe_up_body,
        grid_spec=pltpu.PrefetchScalarGridSpec(
            num_scalar_prefetch=7, grid=(n_steps,),
            in_specs=[pl.BlockSpec(memory_space=pl.ANY),
                      pl.BlockSpec((1, d, fc), wmap), pl.BlockSpec((1, d, fc), wmap)],
            out_specs=pl.BlockSpec((tile, fc), omap),
            scratch_shapes=[pltpu.VMEM((d, fc), BF16), pltpu.VMEM((d, fc), BF16),
                            pltpu.VMEM((2, tile, d), F32), pltpu.SemaphoreType.DMA((2,))]),
        out_shape=jax.ShapeDtypeStruct((rows, de), BF16),
        compiler_params=_params(("arbitrary",)), name="moe_up",
    )(*steps, src, x, w_gate, w_up)
    y = pl.pallas_call(
        _moe_down_body,
        grid_spec=pltpu.PrefetchScalarGridSpec(
            num_scalar_prefetch=6, grid=(n_steps,),
            in_specs=[pl.BlockSpec((tile, de), lambda s, tl, *_: (tl[s], 0)), pl.BlockSpec((1, de, dc), wmap)],
            out_specs=pl.BlockSpec((tile, dc), omap),
            scratch_shapes=[pltpu.VMEM((de, dc), BF16)]),
        out_shape=jax.ShapeDtypeStruct((rows, d), F32),
        compiler_params=_params(("arbitrary",)), name="moe_down",
    )(*steps, act, w_down)
    return y


MM_TM = 1664
MM_TN = 256


def kernel(x_prompt, x_sample, state_hgrn, state_conv, state_ssm, c_prompt, c_sample, lb_params, w_mod, b_mod,
           norm1_g, norm2_g, w_in, hg_norm_g, conv_w, conv_b, dt_bias, a_log, d_skip, ssm_norm_g, w_branch_a,
           w_branch_b, w_out, w_router_group, b_router_group, w_router_expert, b_router_expert, w_gate_e, w_up_e,
           w_down_e, final_g):
    nb, seq, d = x_prompt.shape
    ns = x_sample.shape[0]
    assert w_mod.shape[0] == 1 and x_sample.shape[1] == 1
    n_p = nb * seq
    n = n_p + ns
    nheads = state_hgrn.shape[2]
    hgw = nheads * HG_DK
    n_ssm_heads = state_ssm.shape[2]
    inner = n_ssm_heads * SSM_HEADDIM
    ng = n_ssm_heads // SSM_HPG
    cdim = inner + 2 * ng * SSM_STATE
    col_q, col_f, col_v, col_og = 0, hgw, 2 * hgw, 2 * hgw + d
    col_z = 2 * hgw + 2 * d
    col_x = col_z + inner
    n_main = col_x + cdim
    col_gab = n_main + n_ssm_heads
    row = lambda a: a.reshape(1, -1)

    lb = jnp.cumsum(jax.nn.softmax(lb_params.astype(F32), axis=0), axis=0)[0]

    n_c = ns + nb
    n_c_pad = -(-n_c // 8) * 8
    c_all = jnp.concatenate([c_sample, c_prompt, jnp.zeros((n_c_pad - n_c, d), F32)], axis=0)
    mod = _mm(c_all, w_mod[0], tm=n_c_pad, tn=512, name="mod", a_fn=_silu,
              extras=[(row(b_mod[0]), pl.BlockSpec((1, 512), lambda i, j: (0, j)))],
              epilogue=lambda acc, b: acc + b)
    p_mod = dict(nb=nb, per_row=False, mod_row0=ns)
    s_mod = dict(nb=1, nt=1, tr=ns, per_row=True, mod_row0=0)

    xp2 = x_prompt.reshape(n_p, d)
    xs2 = x_sample.reshape(ns, d)
    h = _norm_mod(xp2, mod, row(norm1_g[0]), nt=seq // 256, tr=256, sc_blk=1, sh_blk=0, into=None, total_rows=n,
                  out_row0=0, name="norm1_prompt", **p_mod)
    h = _norm_mod(xs2, mod, row(norm1_g[0]), sc_blk=1, sh_blk=0, into=h, total_rows=n, out_row0=n_p,
                  name="norm1_sample", **s_mod)

    assert n_ssm_heads == LANES
    u = _mm(h, w_in[0], tm=MM_TM, tn=MM_TN, ncols=n_main, name="in_proj")
    u_t = _mm(h, w_in[0], tm=MM_TM, tn=MM_TN, col0=n_main, ncols=2 * d, name="in_proj_tail")
    u_l = _mm(h, w_in[0], tm=MM_TM, tn=LANES, col0=n_main + 2 * d, ncols=LANES, name="in_proj_last")
    u_dt = u_t[:, :LANES]

    gn_h = row(hg_norm_g[0])
    o_a, hg_p = _hgrn_prompt(u, row(lb), gn_h, nb=nb, seq=seq, nheads=nheads, col_q=col_q, col_f=col_f,
                             col_v=col_v, col_og=col_og, total_rows=n, tb=512)
    f_t = u[n_p:, col_f:col_f + hgw].reshape(ns // STEP_TOKENS, STEP_TOKENS, hgw).transpose(0, 2, 1)
    o_a, hg_s = _hgrn_step(u, f_t, lb.reshape(hgw, 1), gn_h, state_hgrn[0], row0=n_p, nheads=nheads,
                           col_q=col_q, col_v=col_v, col_og=col_og, into=o_a)

    gn_s = row(ssm_norm_g[0])
    y_b, ssm_p = _ssd_prompt(u, u_dt, conv_w[0], row(conv_b[0]), row(dt_bias[0]), row(a_log[0]), row(d_skip[0]),
                             gn_s, nb=nb, seq=seq, col_z=col_z, col_x=col_x, total_rows=n, tb=256)
    xs, bm, cm, xdt, da = _ssd_prep(u, u_dt, state_conv[0].reshape(ns, (SSM_CONV - 1) * cdim), conv_w[0],
                                    row(conv_b[0]), row(dt_bias[0]), row(a_log[0]), row0=n_p, n=ns, col_x=col_x)
    cols = lambda a: a.reshape(ns // STEP_TOKENS, STEP_TOKENS, inner).transpose(0, 2, 1)
    dsk_full = jnp.repeat(d_skip[0], SSM_HEADDIM).reshape(1, inner)
    y_b, ssm_s = _ssd_step(u, xs, bm, cm, cols(xdt), cols(da), dsk_full, gn_s, state_ssm[0], row0=n_p,
                           col_z=col_z, into=y_b)
    xbc = u[:, col_x:col_x + cdim]
    conv_p = xbc[:n_p].reshape(nb, seq, cdim)[:, seq - (SSM_CONV - 1):]
    conv_s = jnp.concatenate([state_conv[0][:, 1:], xbc[n_p:, None, :]], axis=1)

    tile_spec = lambda off: pl.BlockSpec((MM_TM, MM_TN), lambda i, j: (i, j + off // MM_TN))
    last_blk = u_t.shape[1] // LANES - 1
    half_spec = lambda blk0: pl.BlockSpec((MM_TM, LANES), lambda i, j: (i, jnp.minimum(blk0 + 2 * j, last_blk)))
    n_ct = d // MM_TN

    def gate_a(acc, g0, g1):
        return _sigmoid(jnp.concatenate([g0, g1], axis=-1)) * acc

    def gate_b(acc, g0, g1, gl, ta, pb):
        g1 = jnp.where(pl.program_id(1) == n_ct - 1, gl, g1)
        return ta + _sigmoid(jnp.concatenate([g0, g1], axis=-1)) * (pb + acc)

    t_a = _mm(o_a, w_branch_a[0], tm=MM_TM, tn=MM_TN, name="branch_a",
              extras=[(u_t, half_spec(1)), (u_t, half_spec(2))], epilogue=gate_a)
    half = inner // 2
    p_b = _mm(y_b, w_branch_b[0], tm=MM_TM, tn=MM_TN, k=half, name="branch_b_lo")
    gb0 = 1 + d // LANES
    merged = _mm(y_b, w_branch_b[0], tm=MM_TM, tn=MM_TN, k=half, a_kblk=1, b_kblk=1, name="branch_b_hi",
                 extras=[(u_t, half_spec(gb0)), (u_t, half_spec(gb0 + 1)),
                         (u_l, pl.BlockSpec((MM_TM, LANES), lambda i, j: (i, 0))),
                         (t_a, tile_spec(0)), (p_b, tile_spec(0))],
                 epilogue=gate_b, out_dtype=BF16)
    proj = _mm(merged, w_out[0], tm=MM_TM, tn=MM_TN, name="out_proj")

    wr = jnp.concatenate([w_router_group[0], w_router_expert[0],
                          jnp.zeros((d, LANES - N_GROUPS - N_EXPERTS), F32)], axis=1)
    wr3 = jnp.stack(_split3(wr))
    br = jnp.concatenate([b_router_group[0], b_router_expert[0],
                          jnp.zeros((LANES - N_GROUPS - N_EXPERTS,), F32)]).reshape(1, LANES)
    g2n = row(norm2_g[0])
    x1, h2, route = _resid_norm_router(xp2, proj, mod, g2n, wr3, br, nt=seq // 128, tr=128, gate_blk=2, sc_blk=4,
                                       sh_blk=3, proj_row0=0, intos=[None] * 3, total_rows=n,
                                       name="norm2_prompt", **p_mod)
    x1, h2, route = _resid_norm_router(xs2, proj, mod, g2n, wr3, br, gate_blk=2, sc_blk=4, sh_blk=3,
                                       proj_row0=n_p, intos=[x1, h2, route], total_rows=n,
                                       name="norm2_sample", **s_mod)

    e1 = route[:, 2].astype(jnp.int32)
    e2 = route[:, 3].astype(jnp.int32)
    n_tiles = -(-(2 * n + N_EXPERTS * (MOE_TILE - 1)) // MOE_TILE)
    pos, src, steps = _moe_plan(e1, e2, n_tiles)
    y_sorted = _moe_experts(h2, src, w_gate_e[0], w_up_e[0], w_down_e[0], steps, n_tiles)

    fg = row(final_g)
    y_prompt = _final(x1, y_sorted, pos, route, mod, fg, nt=seq // 128, tr=128, gate_blk=5, row0=0, out_rows=n_p,
                      name="final_prompt", **p_mod)
    y_sample = _final(x1, y_sorted, pos, route, mod, fg, gate_blk=5, row0=n_p, out_rows=ns, name="final_sample",
                      **s_mod)

    return (y_prompt.reshape(nb, seq, d), y_sample.reshape(ns, 1, d),
            hg_p[None], conv_p[None], ssm_p.reshape(1, nb, n_ssm_heads, SSM_HEADDIM, SSM_STATE),
            hg_s[None], conv_s[None], ssm_s[None])
```

```python
import functools

import numpy as np
import jax
import jax.numpy as jnp
from jax import lax
from jax.experimental import pallas as pl
from jax.experimental.pallas import tpu as pltpu

F32 = jnp.float32
BF16 = jnp.bfloat16
EPS = 1e-6

VMEM_LIMIT_BYTES = 56 * 1024 * 1024
LANES = 128

HG_DK = 128
HG_CHUNK = 128
HG_HEADS_PER_STEP = 4
SSM_HEADDIM = 64
SSM_STATE = 128
SSM_HPG = 16
SSM_CHUNK = 128
SSM_CONV = 4
N_GROUPS = 4
EXPERTS_PER_GROUP = 4
N_EXPERTS = 16
MOE_TILE = 256
MOE_NCHUNK = 2
SLAB_PITCH = 40


def _params(sem):
    return pltpu.CompilerParams(dimension_semantics=sem, vmem_limit_bytes=VMEM_LIMIT_BYTES)


def _sigmoid(x):
    return 1.0 / (1.0 + jnp.exp(-x))


def _silu(x):
    return x * _sigmoid(x)


def _softplus(x):
    return jnp.maximum(x, 0.0) + jnp.log(1.0 + jnp.exp(-jnp.abs(x)))


def _split3(x):
    p1 = x.astype(BF16)
    r1 = x - p1.astype(F32)
    p2 = r1.astype(BF16)
    r2 = r1 - p2.astype(F32)
    return p1, p2, r2.astype(BF16)


def _dot(a, b):
    return jnp.dot(a, b, preferred_element_type=F32)


def _dot_nt(a, b):
    return lax.dot_general(a, b, (((1,), (1,)), ((), ())), preferred_element_type=F32)


def _dot_tn(a, b):
    return lax.dot_general(a, b, (((0,), (0,)), ((), ())), preferred_element_type=F32)


def _exact_dot(w3_bf16, x_f32):
    return _dot(w3_bf16, jnp.concatenate(_split3(x_f32), axis=0))


def _exact_dot_rhs(x_f32, w3_bf16):
    return _dot(jnp.concatenate(_split3(x_f32), axis=-1), w3_bf16)


def _thrice(w, axis):
    return jnp.concatenate([w, w, w], axis=axis)


def _pcall(body, *, grid, in_specs, args, outs, sem, name, scratch=(), num_prefetch=0):
    n_in = len(args) - num_prefetch
    into = [(i, o[3]) for i, o in enumerate(outs) if o[3] is not None]
    in_specs = list(in_specs) + [pl.BlockSpec(memory_space=pl.ANY)] * len(into)
    all_args = list(args) + [buf for _, buf in into]
    aliases = {len(args) + j: i for j, (i, _) in enumerate(into)}

    def wrapped(*refs):
        keep = refs[:num_prefetch + n_in] + refs[num_prefetch + n_in + len(into):]
        return body(*keep)

    grid_spec = pltpu.PrefetchScalarGridSpec(
        num_scalar_prefetch=num_prefetch, grid=grid, in_specs=in_specs,
        out_specs=[o[2] for o in outs], scratch_shapes=list(scratch))
    res = pl.pallas_call(
        wrapped, grid_spec=grid_spec,
        out_shape=[jax.ShapeDtypeStruct(o[0], o[1]) for o in outs],
        input_output_aliases=aliases, compiler_params=_params(sem), name=name,
    )(*all_args)
    return res


def _mm_body(a_ref, b_ref, *rest, n_extra, a_fn, epilogue):
    extras = rest[:n_extra]
    o_ref = rest[n_extra]
    a = a_ref[...]
    if a_fn is not None:
        a = a_fn(a)
    acc = _dot(a.astype(BF16), b_ref[...].astype(BF16))
    if epilogue is not None:
        acc = epilogue(acc, *[e[...] for e in extras])
    o_ref[...] = acc.astype(o_ref.dtype)


def _mm(a, b, *, tm, tn, name, k=None, a_kblk=0, b_kblk=0, col0=0, ncols=None, extras=(), a_fn=None,
        epilogue=None, out_dtype=F32):
    m = a.shape[0]
    k = a.shape[1] if k is None else k
    ncols = b.shape[1] if ncols is None else ncols
    assert m % tm == 0 and ncols % tn == 0 and col0 % tn == 0
    cb = col0 // tn
    in_specs = [pl.BlockSpec((tm, k), lambda i, j: (i, a_kblk), pipeline_mode=pl.Buffered(1)),
                pl.BlockSpec((k, tn), lambda i, j: (b_kblk, j + cb))]
    in_specs += [s for _, s in extras]
    return pl.pallas_call(
        functools.partial(_mm_body, n_extra=len(extras), a_fn=a_fn, epilogue=epilogue),
        grid=(m // tm, ncols // tn),
        in_specs=in_specs,
        out_specs=pl.BlockSpec((tm, tn), lambda i, j: (i, j)),
        out_shape=jax.ShapeDtypeStruct((m, ncols), out_dtype),
        compiler_params=_params(("parallel", "arbitrary")),
        name=name,
    )(a, b, *[x for x, _ in extras])


def _rms(x, g):
    return x * lax.rsqrt(jnp.mean(x * x, axis=-1, keepdims=True) + EPS) * g


def _mod_view(mod, per_row):
    return mod if per_row else mod.reshape(mod.shape[0], 1, mod.shape[1])


def _mod_spec(per_row, tr, d, col_blk, nt, row0):
    if per_row:
        rb = row0 // tr
        return pl.BlockSpec((tr, d), lambda b, t: (rb + t, col_blk))
    return pl.BlockSpec((1, 1, d), lambda b, t: (row0 + b, 0, col_blk))


def _rows_spec(tr, d, nt, row0):
    rb = row0 // tr
    return pl.BlockSpec((tr, d), lambda b, t: (rb + b * nt + t, 0))


def _norm_mod_body(x_ref, sc_ref, sh_ref, g_ref, o_ref):
    d = x_ref.shape[-1]
    sc = sc_ref[...].reshape(-1, d)
    sh = sh_ref[...].reshape(-1, d)
    o_ref[...] = (_rms(x_ref[...], g_ref[...]) * (1.0 + sc) + sh).astype(o_ref.dtype)


def _norm_mod(x, mod, g, *, nb, nt, tr, per_row, mod_row0, sc_blk, sh_blk, into, total_rows, out_row0, name):
    d = x.shape[-1]
    mv = _mod_view(mod, per_row)
    return _pcall(
        _norm_mod_body, grid=(nb, nt),
        in_specs=[_rows_spec(tr, d, nt, 0),
                  _mod_spec(per_row, tr, d, sc_blk, nt, mod_row0),
                  _mod_spec(per_row, tr, d, sh_blk, nt, mod_row0),
                  pl.BlockSpec((1, d), lambda b, t: (0, 0))],
        args=[x, mv, mv, g],
        outs=[((total_rows, d), BF16, _rows_spec(tr, d, nt, out_row0), into)],
        sem=("parallel", "arbitrary"), name=name)[0]


def _router_math(lg):
    lane = lax.broadcasted_iota(jnp.int32, lg.shape, 1)
    neg = jnp.float32(-jnp.inf)
    gm = jnp.where(lane < N_GROUPS, lg, neg)
    gmax = jnp.max(gm, axis=-1, keepdims=True)
    gidx = jnp.min(jnp.where(gm == gmax, lane, LANES), axis=-1, keepdims=True)
    gsum = jnp.sum(jnp.where(lane < N_GROUPS, jnp.exp(gm - gmax), 0.0), axis=-1, keepdims=True)
    p_group = 1.0 / gsum
    lo = N_GROUPS + EXPERTS_PER_GROUP * gidx
    em = jnp.where((lane >= lo) & (lane < lo + EXPERTS_PER_GROUP), lg, neg)
    m1 = jnp.max(em, axis=-1, keepdims=True)
    i1 = jnp.min(jnp.where(em == m1, lane, LANES), axis=-1, keepdims=True)
    em2 = jnp.where(lane == i1, neg, em)
    m2 = jnp.max(em2, axis=-1, keepdims=True)
    i2 = jnp.min(jnp.where(em2 == m2, lane, LANES), axis=-1, keepdims=True)
    r = jnp.exp(m2 - m1)
    w1 = p_group / (1.0 + r)
    w2 = p_group * r / (1.0 + r)
    e1 = (i1 - N_GROUPS).astype(F32)
    e2 = (i2 - N_GROUPS).astype(F32)
    out = jnp.where(lane == 0, w1, 0.0)
    out = jnp.where(lane == 1, w2, out)
    out = jnp.where(lane == 2, e1, out)
    out = jnp.where(lane == 3, e2, out)
    return out


def _resid_norm_router_body(x_ref, p_ref, gt_ref, sc_ref, sh_ref, g_ref, wr_ref, br_ref, x1_ref, h_ref, rt_ref):
    d = x_ref.shape[-1]
    gt = gt_ref[...].reshape(-1, d)
    sc = sc_ref[...].reshape(-1, d)
    sh = sh_ref[...].reshape(-1, d)
    x1 = x_ref[...] + gt * p_ref[...]
    x1_ref[...] = x1
    h = _rms(x1, g_ref[...]) * (1.0 + sc) + sh
    tr = x_ref.shape[0]
    for k in range(SLAB_PITCH):
        slab = h[:, k * LANES:(k + 1) * LANES] if k < d // LANES else jnp.zeros((tr, LANES), F32)
        h_ref[pl.ds(k, tr, stride=SLAB_PITCH), :] = slab
    h1, h2, _ = _split3(h)
    lg = _dot(jnp.concatenate([h1, h1, h2], axis=-1), wr_ref[...])
    rt_ref[...] = _router_math(lg + br_ref[...])


def _resid_norm_router(x, proj, mod, g, wr3, br, *, nb, nt, tr, per_row, mod_row0, gate_blk, sc_blk, sh_blk,
                       proj_row0, intos, total_rows, name):
    d = x.shape[-1]
    mv = _mod_view(mod, per_row)
    ms = lambda blk: _mod_spec(per_row, tr, d, blk, nt, mod_row0)
    orow = _rows_spec(tr, d, nt, proj_row0)
    return _pcall(
        _resid_norm_router_body, grid=(nb, nt),
        in_specs=[_rows_spec(tr, d, nt, 0), orow, ms(gate_blk), ms(sc_blk), ms(sh_blk),
                  pl.BlockSpec((1, d), lambda b, t: (0, 0)),
                  pl.BlockSpec((3 * d, LANES), lambda b, t: (0, 0)),
                  pl.BlockSpec((1, LANES), lambda b, t: (0, 0))],
        args=[x, proj, mv, mv, mv, g, wr3, br],
        outs=[((total_rows, d), F32, orow, intos[0]),
              ((total_rows * SLAB_PITCH, LANES), F32, _rows_spec(tr * SLAB_PITCH, LANES, nt, proj_row0 * SLAB_PITCH),
               intos[1]),
              ((total_rows, LANES), F32, _rows_spec(tr, LANES, nt, proj_row0), intos[2])],
        sem=("parallel", "arbitrary"), name=name)


def _row_gather_start(src_hbm, idx_ref, idx0, dst, sem, n, unrolled=False, rows=1, pitch=1):
    def body(r, carry):
        s0 = idx_ref[idx0 + r] * pitch
        d0 = r * pitch
        if pitch % 8 == 0:
            s0 = pl.multiple_of(s0, 8)
            d0 = d0 if isinstance(d0, int) else pl.multiple_of(d0, 8)
        pltpu.make_async_copy(src_hbm.at[pl.ds(s0, rows), :], dst.at[pl.ds(d0, rows), :], sem).start()
        return carry

    if unrolled:
        for r in range(n):
            body(r, 0)
    else:
        lax.fori_loop(0, n, body, 0, unroll=8)


def _row_gather_wait(src_hbm, dst, sem, n, rows=1, pitch=1):
    def body(r, carry):
        d0 = pl.multiple_of(r * pitch, 8) if pitch % 8 == 0 else r * pitch
        pltpu.make_async_copy(src_hbm.at[pl.ds(0, rows), :], dst.at[pl.ds(d0, rows), :], sem).wait()
        return carry

    lax.fori_loop(0, n, body, 0, unroll=8)


def _final_body(pos_ref, x1_ref, rt_ref, gt_ref, g_ref, y_hbm, o_ref, ybuf, sem, *, tr, tok0, n_tok):
    d = x1_ref.shape[-1]
    i = pl.program_id(0)
    slot = lax.rem(i, 2)

    def start(step, sl):
        for k in range(2):
            _row_gather_start(y_hbm, pos_ref, k * n_tok + tok0 + step * tr, ybuf.at[sl, k], sem.at[sl], tr)

    @pl.when(i == 0)
    def _():
        start(0, 0)

    @pl.when(i + 1 < pl.num_programs(0))
    def _():
        start(i + 1, 1 - slot)

    for k in range(2):
        _row_gather_wait(y_hbm, ybuf.at[slot, k], sem.at[slot], tr)
    gt = gt_ref[...].reshape(-1, d)
    rt = rt_ref[...]
    moe = rt[:, 0:1] * ybuf[slot, 0] + rt[:, 1:2] * ybuf[slot, 1]
    o_ref[...] = _rms(x1_ref[...] + gt * moe, g_ref[...])


def _final(x1, y_sorted, pos, route, mod, g, *, nb, nt, tr, per_row, mod_row0, gate_blk, row0, out_rows, name):
    d = x1.shape[-1]
    n_tok = x1.shape[0]
    rb = row0 // tr
    if per_row:
        mspec = pl.BlockSpec((tr, d), lambda i, pos_: (mod_row0 // tr + i, gate_blk))
    else:
        mspec = pl.BlockSpec((1, 1, d), lambda i, pos_: (mod_row0 + i // nt, 0, gate_blk))
    return _pcall(
        functools.partial(_final_body, tr=tr, tok0=row0, n_tok=n_tok), grid=(nb * nt,),
        in_specs=[pl.BlockSpec((tr, d), lambda i, pos_: (rb + i, 0)),
                  pl.BlockSpec((tr, LANES), lambda i, pos_: (rb + i, 0)),
                  mspec,
                  pl.BlockSpec((1, d), lambda i, pos_: (0, 0)),
                  pl.BlockSpec(memory_space=pl.ANY)],
        args=[pos, x1, route, _mod_view(mod, per_row), g, y_sorted],
        outs=[((out_rows, d), F32, pl.BlockSpec((tr, d), lambda i, pos_: (i, 0)), None)],
        sem=("arbitrary",), name=name, num_prefetch=1,
        scratch=[pltpu.VMEM((2, 2, tr, d), F32), pltpu.SemaphoreType.DMA((2,))])[0]


HG_MXU_LEVELS = 3


def _hgrn_tables(c):
    nlev = int(np.log2(c))
    t = np.arange(c)[:, None]
    u = np.arange(c)[None, :]
    mats = [u <= t]
    masks = []
    for l in range(nlev):
        m = 1 << l
        p = (t // (2 * m)) * (2 * m) + m - 1
        upper = ((t >> l) & 1) == 1
        if l < HG_MXU_LEVELS:
            mats.append(np.where(upper, (u > p) & (u <= t), (u > t) & (u <= p)))
        same = (t >> (l + 1)) == (u >> (l + 1))
        masks.append(same & upper & (((u >> l) & 1) == 0))
    masks.append(t == u)
    w = np.concatenate([m.astype(np.float32) for m in mats], axis=0)
    return jnp.asarray(np.tile(w, (1, 3)), BF16), jnp.asarray(np.stack(masks).astype(np.float32))


def _hgrn_prompt_body(q_ref, f_ref, v_ref, og_ref, lb_ref, gn_ref, w_ref, mk_ref, o_ref, s_ref, st_scr, *,
                      nchunks, nlev):
    c = HG_CHUNK
    hp = st_scr.shape[0]
    t = pl.program_id(2)

    @pl.when(t == 0)
    def _():
        st_scr[...] = jnp.zeros_like(st_scr)

    lb = lb_ref[...]
    gn = gn_ref[...]
    trow = lax.broadcasted_iota(jnp.int32, (c, HG_DK), 0)
    upper = [((trow >> l) & 1) == 1 for l in range(HG_MXU_LEVELS, nlev)]

    def chunk(ci, carry):
        r0 = pl.multiple_of(ci * c, c)
        rows = pl.ds(r0, c)
        for i in range(hp):
            ln = slice(i * HG_DK, (i + 1) * HG_DK)
            f = lb[:, ln] + (1.0 - lb[:, ln]) * _sigmoid(f_ref[rows, ln])
            g = _exact_dot(w_ref[...], jnp.log2(f))
            kk = 1.0 - f
            q = _silu(q_ref[rows, ln])
            v = v_ref[rows, ln].astype(BF16)
            b = g[0:c]
            b_end = b[c - 1:c, :]
            st = st_scr[i]
            o = _dot_nt((q * jnp.exp2(b)).astype(BF16), st.astype(BF16))
            scores = mk_ref[nlev] * _dot_nt(q.astype(BF16), kk.astype(BF16))
            for l in range(nlev):
                if l < HG_MXU_LEVELS:
                    x = jnp.exp2(g[(1 + l) * c:(2 + l) * c])
                else:
                    m = 1 << l
                    b3 = b.reshape(c // (2 * m), 2 * m, HG_DK)
                    d = (b3 - b3[:, m - 1:m, :]).reshape(c, HG_DK)
                    x = jnp.exp2(jnp.where(upper[l - HG_MXU_LEVELS], d, -d))
                scores = scores + mk_ref[l] * _dot_nt((q * x).astype(BF16), (kk * x).astype(BF16))
            o = o + _dot(scores.astype(BF16), v)
            kw = (kk * jnp.exp2(b_end - b)).astype(BF16)
            st_scr[i] = st * jnp.exp2(b_end) + _dot_tn(v, kw)
            o = _rms(o, gn[:, ln]) * _silu(og_ref[rows, ln])
            o_ref[rows, ln] = o.astype(o_ref.dtype)
        return carry

    lax.fori_loop(0, nchunks, chunk, 0)

    @pl.when(t == pl.num_programs(2) - 1)
    def _():
        for i in range(hp):
            s_ref[0, i] = st_scr[i].T


def _hgrn_prompt(u, lb, gn, *, nb, seq, nheads, col_q, col_f, col_v, col_og, total_rows, tb):
    c = HG_CHUNK
    nt = seq // tb
    nlev = int(np.log2(c))
    w, mk = _hgrn_tables(c)
    hp = HG_HEADS_PER_STEP
    wd = hp * HG_DK
    cs = lambda col: pl.BlockSpec((tb, wd), lambda b, h, t: (b * nt + t, col // wd + h))
    hv = pl.BlockSpec((1, wd), lambda b, h, t: (0, h))
    return _pcall(
        functools.partial(_hgrn_prompt_body, nchunks=tb // c, nlev=nlev),
        grid=(nb, nheads // hp, nt),
        in_specs=[cs(col_q), cs(col_f), cs(col_v), cs(col_og), hv, hv,
                  pl.BlockSpec(w.shape, lambda b, h, t: (0, 0)),
                  pl.BlockSpec(mk.shape, lambda b, h, t: (0, 0, 0))],
        args=[u, u, u, u, lb, gn, w, mk],
        outs=[((total_rows, nheads * HG_DK), BF16, pl.BlockSpec((tb, wd), lambda b, h, t: (b * nt + t, h)), None),
              ((nb, nheads, HG_DK, HG_DK), F32, pl.BlockSpec((1, hp, HG_DK, HG_DK), lambda b, h, t: (b, h, 0, 0)), None)],
        sem=("parallel", "parallel", "arbitrary"), name="hgrn_prompt",
        scratch=[pltpu.VMEM((hp, HG_DK, HG_DK), F32)])


def _ssm_expand_tables(n_heads):
    ng = n_heads // SSM_HPG
    w2, w1 = SSM_HPG * LANES, SSM_HPG * SSM_HEADDIM
    e = np.zeros((ng, n_heads, w2 + w1), np.float32)
    for g in range(ng):
        for i in range(SSM_HPG):
            e[g, g * SSM_HPG + i, i * LANES:(i + 1) * LANES] = 1.0
            e[g, g * SSM_HPG + i, w2 + i * SSM_HEADDIM:w2 + (i + 1) * SSM_HEADDIM] = 1.0
    return jnp.asarray(np.tile(e, (1, 3, 1)), BF16)


def _conv_silu(buf_ref, w, bias, r0, n):
    b = buf_ref[pl.ds(r0, n + 8), :]
    acc = b * w[0:1, :]
    for j in range(1, SSM_CONV):
        acc = pltpu.roll(acc, 1, axis=0) + b * w[j:j + 1, :]
    return _silu(acc[8:, :] + bias)


def _ssd_prompt_body(x_ref, b_ref, c_ref, z_ref, dt_ref, wx_ref, wb_ref, wc_ref, bx_ref, bb_ref, bc_ref,
                     dtb_ref, alog_ref, dsk_ref, e_ref, gn_ref, y_ref, s_ref,
                     ht_scr, cbx, cbb, cbc, *, nchunks):
    L = SSM_CHUNK
    tb = nchunks * L
    w2 = SSM_HPG * LANES
    t = pl.program_id(2)

    @pl.when(t == 0)
    def _():
        ht_scr[...] = jnp.zeros_like(ht_scr)
        cbx[0:8, :] = jnp.zeros((8, cbx.shape[1]), F32)
        cbb[0:8, :] = jnp.zeros((8, cbb.shape[1]), F32)
        cbc[0:8, :] = jnp.zeros((8, cbc.shape[1]), F32)

    cbx[8:8 + tb, :] = x_ref[...]
    cbb[8:8 + tb, :] = b_ref[...]
    cbc[8:8 + tb, :] = c_ref[...]

    e_all = e_ref[0]
    e1 = e_all[:, w2:]
    neg_a = -jnp.exp(alog_ref[...])
    dsk1 = _exact_dot_rhs(dsk_ref[...], e1)
    row = lax.broadcasted_iota(jnp.int32, (L, L), 0)
    col = lax.broadcasted_iota(jnp.int32, (L, L), 1)
    causal = col <= row
    tril = _thrice(jnp.where(causal, 1.0, 0.0).astype(BF16), 1)
    lane = lax.broadcasted_iota(jnp.int32, (L, LANES), 1)
    lo_half = lane < SSM_HEADDIM

    for ci in range(nchunks):
        r0 = ci * L
        xs = _conv_silu(cbx, wx_ref[...], bx_ref[...], r0, L)
        bm = _conv_silu(cbb, wb_ref[...], bb_ref[...], r0, L).astype(BF16)
        cm = _conv_silu(cbc, wc_ref[...], bc_ref[...], r0, L).astype(BF16)
        dt = _softplus(dt_ref[r0:r0 + L, :] + dtb_ref[...])
        cum = _exact_dot(tril, dt * neg_a)
        cum_e = _exact_dot_rhs(cum, e_all)
        cum1 = cum_e[:, w2:]
        dt1 = _exact_dot_rhs(dt, e1)
        xdt = xs * dt1
        cum_end = cum1[L - 1:L, :]
        ht = ht_scr[...]
        y = _dot(cm, ht.astype(BF16)) * jnp.exp(cum1)
        cb = _dot_nt(cm, bm)
        parts = []
        for j in range(SSM_HPG // 2):
            xp = xdt[:, j * LANES:(j + 1) * LANES]
            acc = None
            for half in range(2):
                i = 2 * j + half
                colb = cum_e[:, i * LANES:(i + 1) * LANES]
                seg = colb - colb.T
                m = (cb * jnp.exp(jnp.where(causal, seg, -jnp.inf))).astype(BF16)
                xh = jnp.where(lo_half if half == 0 else ~lo_half, xp, 0.0).astype(BF16)
                d = _dot(m, xh)
                acc = d if acc is None else acc + d
            parts.append(acc)
        y = y + jnp.concatenate(parts, axis=-1) + dsk1 * xs
        y = y * _silu(z_ref[r0:r0 + L, :])
        y_ref[r0:r0 + L, :] = _rms(y, gn_ref[...]).astype(y_ref.dtype)
        xw = (xdt * jnp.exp(cum_end - cum1)).astype(BF16)
        ht_scr[...] = ht * jnp.exp(cum_end) + _dot_tn(bm, xw)

    cbx[0:8, :] = cbx[tb:tb + 8, :]
    cbb[0:8, :] = cbb[tb:tb + 8, :]
    cbc[0:8, :] = cbc[tb:tb + 8, :]

    @pl.when(t == pl.num_programs(2) - 1)
    def _():
        s_ref[0, 0] = ht_scr[...].T


def _ssd_prompt(u, u_dt, conv_w, conv_b, dt_bias, a_log, d_skip, gn, *, nb, seq, col_z, col_x, total_rows, tb):
    n_heads = u_dt.shape[1]
    ng = n_heads // SSM_HPG
    w1 = SSM_HPG * SSM_HEADDIM
    inner = ng * w1
    nt = seq // tb
    e = _ssm_expand_tables(n_heads)
    xb, bb, cb_, zb = col_x // w1, (col_x + inner) // LANES, (col_x + inner + ng * SSM_STATE) // LANES, col_z // w1
    wide = lambda blk0: pl.BlockSpec((tb, w1), lambda b, g, t: (b * nt + t, blk0 + g))
    thin = lambda blk0: pl.BlockSpec((tb, LANES), lambda b, g, t: (b * nt + t, blk0 + g))
    cw = lambda rows, width, blk0: pl.BlockSpec((rows, width), lambda b, g, t: (0, blk0 + g))
    full = lambda a: pl.BlockSpec(a.shape, lambda b, g, t: (0,) * a.ndim)
    return _pcall(
        functools.partial(_ssd_prompt_body, nchunks=tb // SSM_CHUNK),
        grid=(nb, ng, nt),
        in_specs=[wide(xb), thin(bb), thin(cb_), wide(zb),
                  pl.BlockSpec((tb, n_heads), lambda b, g, t: (b * nt + t, 0)),
                  cw(SSM_CONV, w1, 0), cw(SSM_CONV, LANES, inner // LANES),
                  cw(SSM_CONV, LANES, (inner + ng * SSM_STATE) // LANES),
                  cw(1, w1, 0), cw(1, LANES, inner // LANES), cw(1, LANES, (inner + ng * SSM_STATE) // LANES),
                  full(dt_bias), full(a_log), full(d_skip),
                  pl.BlockSpec((1,) + e.shape[1:], lambda b, g, t: (g, 0, 0)),
                  cw(1, w1, 0)],
        args=[u, u, u, u, u_dt, conv_w, conv_w, conv_w, conv_b, conv_b, conv_b, dt_bias, a_log, d_skip, e, gn],
        outs=[((total_rows, inner), BF16, pl.BlockSpec((tb, w1), lambda b, g, t: (b * nt + t, g)), None),
              ((nb, ng, w1, SSM_STATE), F32, pl.BlockSpec((1, 1, w1, SSM_STATE), lambda b, g, t: (b, g, 0, 0)), None)],
        sem=("parallel", "parallel", "arbitrary"), name="ssd_prompt",
        scratch=[pltpu.VMEM((SSM_STATE, w1), F32), pltpu.VMEM((tb + 8, w1), F32),
                 pltpu.VMEM((tb + 8, LANES), F32), pltpu.VMEM((tb + 8, LANES), F32)])


STEP_TOKENS = 16
STEP_HEADS = 4


def _hgrn_step_body(q_ref, v_ref, og_ref, ft_ref, lbc_ref, gn_ref, s_ref, o_ref, so_ref):
    lbc = lbc_ref[...]
    f = lbc + (1.0 - lbc) * _sigmoid(ft_ref[0])
    kk = 1.0 - f
    q = _silu(q_ref[...])
    v = v_ref[...]
    gn = gn_ref[...]
    og = og_ref[...]
    for i in range(STEP_HEADS):
        ln = slice(i * HG_DK, (i + 1) * HG_DK)
        rows = []
        for j in range(STEP_TOKENS):
            sn = f[ln, j:j + 1] * s_ref[j, i] + kk[ln, j:j + 1] * v[j:j + 1, ln]
            so_ref[j, i] = sn
            qj = jnp.broadcast_to(q[j:j + 1, ln], (16, HG_DK)).astype(BF16)
            rows.append(_dot(qj, sn.astype(BF16))[0:1, :])
        o = jnp.concatenate(rows, axis=0)
        o_ref[:, ln] = (_rms(o, gn[:, ln]) * _silu(og[:, ln])).astype(o_ref.dtype)


def _hgrn_step(u, f_t, lb_col, gn, state, *, row0, nheads, col_q, col_v, col_og, into):
    n = state.shape[0]
    tb = STEP_TOKENS
    hp = STEP_HEADS
    wd = hp * HG_DK
    rb = row0 // tb
    cs = lambda col: pl.BlockSpec((tb, wd), lambda b, h: (rb + b, col // wd + h))
    sspec = pl.BlockSpec((tb, hp, HG_DK, HG_DK), lambda b, h: (b, h, 0, 0))
    return _pcall(
        _hgrn_step_body, grid=(n // tb, nheads // hp),
        in_specs=[cs(col_q), cs(col_v), cs(col_og),
                  pl.BlockSpec((1, wd, tb), lambda b, h: (b, h, 0)),
                  pl.BlockSpec((wd, 1), lambda b, h: (h, 0)),
                  pl.BlockSpec((1, wd), lambda b, h: (0, h)),
                  sspec],
        args=[u, u, u, f_t, lb_col, gn, state],
        outs=[(into.shape, into.dtype, pl.BlockSpec((tb, wd), lambda b, h: (rb + b, h)), into),
              (state.shape, F32, sspec, None)],
        sem=("parallel", "arbitrary"), name="hgrn_step")


def _ssd_prep_body(x_ref, b_ref, c_ref, sx0, sx1, sx2, sb0, sb1, sb2, sc0, sc1, sc2, dt_ref,
                   wx_ref, wb_ref, wc_ref, bx_ref, bb_ref, bc_ref, dtb_ref, alog_ref, e_ref,
                   xs_ref, bm_ref, cm_ref, xdt_ref, da_ref):
    def conv(taps, cur, w_ref, bias_ref):
        w = w_ref[...]
        acc = bias_ref[...] + cur[...] * w[SSM_CONV - 1:SSM_CONV, :]
        for j, tap in enumerate(taps):
            acc = acc + tap[...] * w[j:j + 1, :]
        return _silu(acc)

    xs = conv((sx0, sx1, sx2), x_ref, wx_ref, bx_ref)
    xs_ref[...] = xs
    bm_ref[...] = conv((sb0, sb1, sb2), b_ref, wb_ref, bb_ref)
    cm_ref[...] = conv((sc0, sc1, sc2), c_ref, wc_ref, bc_ref)
    e1 = e_ref[0][:, SSM_HPG * LANES:]
    dt = _softplus(dt_ref[...] + dtb_ref[...])
    xdt_ref[...] = xs * _exact_dot_rhs(dt, e1)
    da_ref[...] = _exact_dot_rhs(jnp.exp(dt * -jnp.exp(alog_ref[...])), e1)


def _ssd_prep(u, u_dt, conv_state2d, conv_w, conv_b, dt_bias, a_log, *, row0, n, col_x):
    n_heads = u_dt.shape[1]
    ng = n_heads // SSM_HPG
    w1 = SSM_HPG * SSM_HEADDIM
    inner = ng * w1
    cdim = inner + 2 * ng * SSM_STATE
    e = _ssm_expand_tables(n_heads)
    rb = row0 // n
    xoff, boff, coff = 0, inner, inner + ng * SSM_STATE
    cur = lambda width, off: pl.BlockSpec((n, width), lambda g: (rb, (col_x + off) // width + g))
    tap = lambda width, off, j: pl.BlockSpec((n, width), lambda g: (0, (j * cdim + off) // width + g))
    cw = lambda rows, width, off: pl.BlockSpec((rows, width), lambda g: (0, off // width + g))
    full = lambda a: pl.BlockSpec(a.shape, lambda g: (0,) * a.ndim)
    taps = [tap(w1, xoff, j) for j in range(3)] + [tap(LANES, boff, j) for j in range(3)] + \
           [tap(LANES, coff, j) for j in range(3)]
    o_wide = pl.BlockSpec((n, w1), lambda g: (0, g))
    o_thin = pl.BlockSpec((n, LANES), lambda g: (0, g))
    return _pcall(
        _ssd_prep_body, grid=(ng,),
        in_specs=[cur(w1, xoff), cur(LANES, boff), cur(LANES, coff)] + taps +
                 [pl.BlockSpec((n, n_heads), lambda g: (rb, 0)),
                  cw(SSM_CONV, w1, xoff), cw(SSM_CONV, LANES, boff), cw(SSM_CONV, LANES, coff),
                  cw(1, w1, xoff), cw(1, LANES, boff), cw(1, LANES, coff),
                  full(dt_bias), full(a_log),
                  pl.BlockSpec((1,) + e.shape[1:], lambda g: (g, 0, 0))],
        args=[u, u, u] + [conv_state2d] * 9 + [u_dt, conv_w, conv_w, conv_w, conv_b, conv_b, conv_b,
                                               dt_bias, a_log, e],
        outs=[((n, inner), F32, o_wide, None), ((n, ng * SSM_STATE), F32, o_thin, None),
              ((n, ng * SSM_STATE), F32, o_thin, None), ((n, inner), F32, o_wide, None),
              ((n, inner), F32, o_wide, None)],
        sem=("arbitrary",), name="ssd_prep")


def _ssd_step_body(xs_ref, bm_ref, cm_ref, z_ref, xdt_ref, da_ref, dsk_ref, gn_ref, s_ref, y_ref, so_ref):
    w1 = SSM_HPG * SSM_HEADDIM
    bm = bm_ref[...]
    cm = cm_ref[...]
    xdt = xdt_ref[0]
    da = da_ref[0]
    rows = []
    for j in range(STEP_TOKENS):
        h = s_ref[j].reshape(w1, SSM_STATE)
        hn = da[:, j:j + 1] * h + xdt[:, j:j + 1] * bm[j:j + 1, :]
        so_ref[j] = hn.reshape(SSM_HPG, SSM_HEADDIM, SSM_STATE)
        cj = jnp.broadcast_to(cm[j:j + 1, :], (16, SSM_STATE)).astype(BF16)
        rows.append(_dot_nt(cj, hn.astype(BF16))[0:1, :])
    y = jnp.concatenate(rows, axis=0) + dsk_ref[...] * xs_ref[...]
    y = y * _silu(z_ref[...])
    y_ref[...] = _rms(y, gn_ref[...]).astype(y_ref.dtype)


def _ssd_step(u, xs, bm, cm, xdt_t, da_t, dsk_full, gn, state, *, row0, col_z, into):
    n, n_heads = state.shape[0], state.shape[1]
    ng = n_heads // SSM_HPG
    w1 = SSM_HPG * SSM_HEADDIM
    tb = STEP_TOKENS
    rb = row0 // tb
    sspec = pl.BlockSpec((tb, SSM_HPG, SSM_HEADDIM, SSM_STATE), lambda b, g: (b, g, 0, 0))
    colspec = pl.BlockSpec((1, w1, tb), lambda b, g: (b, g, 0))
    return _pcall(
        _ssd_step_body, grid=(n // tb, ng),
        in_specs=[pl.BlockSpec((tb, w1), lambda b, g: (b, g)),
                  pl.BlockSpec((tb, SSM_STATE), lambda b, g: (b, g)),
                  pl.BlockSpec((tb, SSM_STATE), lambda b, g: (b, g)),
                  pl.BlockSpec((tb, w1), lambda b, g: (rb + b, col_z // w1 + g)),
                  colspec, colspec,
                  pl.BlockSpec((1, w1), lambda b, g: (0, g)),
                  pl.BlockSpec((1, w1), lambda b, g: (0, g)),
                  sspec],
        args=[xs, bm, cm, u, xdt_t, da_t, dsk_full, gn, state],
        outs=[(into.shape, into.dtype, pl.BlockSpec((tb, w1), lambda b, g: (rb + b, g)), into),
              (state.shape, F32, sspec, None)],
        sem=("parallel", "arbitrary"), name="ssd_step")


def _moe_plan(e1, e2, n_tiles):
    n = e1.shape[0]
    tile = MOE_TILE
    eid = jnp.concatenate([e1, e2]).astype(jnp.int32)
    onehot = (eid[:, None] == jnp.arange(N_EXPERTS, dtype=jnp.int32)[None, :]).astype(jnp.int32)
    blk = 128
    assert (2 * n) % blk == 0
    oh3 = onehot.astype(F32).reshape(2 * n // blk, blk, N_EXPERTS)
    tri = jnp.tril(jnp.ones((blk, blk), F32), -1)
    inblk = jnp.einsum("ts,bse->bte", tri, oh3, preferred_element_type=F32)
    btot = jnp.sum(oh3, axis=1)
    before = (jnp.cumsum(btot, axis=0) - btot)[:, None, :] + inblk
    rank = jnp.sum(before.reshape(2 * n, N_EXPERTS) * onehot, axis=1).astype(jnp.int32)
    counts = jnp.sum(onehot, axis=0)
    ptiles = (counts + tile - 1) // tile
    tstart = jnp.cumsum(ptiles) - ptiles
    pos = (tstart * tile)[eid] + rank
    tok = jnp.concatenate([jnp.arange(n, dtype=jnp.int32)] * 2)
    src = jnp.zeros((n_tiles * tile,), jnp.int32).at[pos].set(tok)
    n_used = jnp.sum(ptiles)
    nc = MOE_NCHUNK
    s = jnp.arange(nc * n_tiles, dtype=jnp.int32)
    send = jnp.cumsum(ptiles) * nc
    s_eff = jnp.minimum(s, nc * n_used - 1)
    ex = jnp.searchsorted(send, s_eff, side="right").astype(jnp.int32)
    local = s_eff - (send - ptiles * nc)[ex]
    pt = jnp.maximum(ptiles[ex], 1)
    chunk = local // pt
    tl = tstart[ex] + local % pt
    active = (s < nc * n_used).astype(jnp.int32)
    idle = s - nc * n_used
    tl = jnp.where(active == 1, tl, n_used + idle // nc)
    ock = jnp.where(active == 1, chunk, idle % nc)
    wid = ex * nc + chunk
    first = jnp.concatenate([jnp.ones((1,), jnp.int32), (wid[1:] != wid[:-1]).astype(jnp.int32)])
    i32 = lambda a: a.astype(jnp.int32)
    return i32(pos), src, (i32(tl), i32(ex), i32(chunk), i32(ock), first, active)


def _moe_up_body(tl_ref, ex_ref, wck_ref, ock_ref, first_ref, act_ref, src_ref, x_hbm, wg_ref, wu_ref, o_ref,
                 wg_s, wu_s, xbuf, sem):
    s = pl.program_id(0)
    last = pl.num_programs(0) - 1
    slot = lax.rem(s, 2)
    tile = o_ref.shape[0]
    n_slab = wg_s.shape[0] // LANES
    slab = dict(rows=n_slab, pitch=SLAB_PITCH)
    nxt = jnp.minimum(s + 1, last)

    def start(step, sl, unrolled=False):
        _row_gather_start(x_hbm, src_ref, tl_ref[step] * tile, xbuf.at[sl], sem.at[sl], tile, unrolled, **slab)

    @pl.when(s == 0)
    def _():
        start(0, 0)

    @pl.when(first_ref[s] == 1)
    def _():
        wg_s[...] = wg_ref[0].astype(BF16)
        wu_s[...] = wu_ref[0].astype(BF16)

    _row_gather_wait(x_hbm, xbuf.at[slot], sem.at[slot], tile, **slab)

    @pl.when(act_ref[s] == 1)
    def _():
        xb = xbuf.at[slot]
        x = jnp.concatenate([xb[pl.ds(k, tile, stride=SLAB_PITCH), :] for k in range(n_slab)], axis=-1)
        x = x.astype(BF16)
        start(nxt, 1 - slot, unrolled=True)
        o_ref[...] = (_silu(_dot(x, wg_s[...])) * _dot(x, wu_s[...])).astype(o_ref.dtype)

    @pl.when(act_ref[s] == 0)
    def _():
        start(nxt, 1 - slot)
        o_ref[...] = jnp.zeros_like(o_ref)

    @pl.when(s == last)
    def _():
        _row_gather_wait(x_hbm, xbuf.at[1 - slot], sem.at[1 - slot], tile, **slab)


def _moe_down_body(tl_ref, ex_ref, wck_ref, ock_ref, first_ref, act_ref, a_ref, wd_ref, o_ref, wd_s):
    s = pl.program_id(0)

    @pl.when(first_ref[s] == 1)
    def _():
        wd_s[...] = wd_ref[0].astype(BF16)

    @pl.when(act_ref[s] == 1)
    def _():
        o_ref[...] = _dot(a_ref[...], wd_s[...])

    @pl.when(act_ref[s] == 0)
    def _():
        o_ref[...] = jnp.zeros_like(o_ref)


def _moe_experts(x, src, w_gate, w_up, w_down, steps, n_tiles):
    tile = MOE_TILE
    d, de = w_gate.shape[1], w_gate.shape[2]
    nc = MOE_NCHUNK
    fc, dc = de // nc, d // nc
    n_steps = nc * n_tiles
    rows = n_tiles * tile
    wmap = lambda s, tl, ex, wck, ock, fi, ac, *_: (ex[s], 0, wck[s])
    omap = lambda s, tl, ex, wck, ock, fi, ac, *_: (tl[s], ock[s])
    act = pl.pallas_call(
        _moe_up_body,
        grid_spec=pltpu.PrefetchScalarGridSpec(
            num_scalar_prefetch=7, grid=(n_steps,),
            in_specs=[pl.BlockSpec(memory_space=pl.ANY),
                      pl.BlockSpec((1, d, fc), wmap), pl.BlockSpec((1, d, fc), wmap)],
            out_specs=pl.BlockSpec((tile, fc), omap),
            scratch_shapes=[pltpu.VMEM((d, fc), BF16), pltpu.VMEM((d, fc), BF16),
                            pltpu.VMEM((2, tile * SLAB_PITCH, LANES), F32), pltpu.SemaphoreType.DMA((2,))]),
        out_shape=jax.ShapeDtypeStruct((rows, de), BF16),
        compiler_params=_params(("arbitrary",)), name="moe_up",
    )(*steps, src, x, w_gate, w_up)
    y = pl.pallas_call(
        _moe_down_body,
        grid_spec=pltpu.PrefetchScalarGridSpec(
            num_scalar_prefetch=6, grid=(n_steps,),
            in_specs=[pl.BlockSpec((tile, de), lambda s, tl, *_: (tl[s], 0)), pl.BlockSpec((1, de, dc), wmap)],
            out_specs=pl.BlockSpec((tile, dc), omap),
            scratch_shapes=[pltpu.VMEM((de, dc), BF16)]),
        out_shape=jax.ShapeDtypeStruct((rows, d), F32),
        compiler_params=_params(("arbitrary",)), name="moe_down",
    )(*steps, act, w_down)
    return y


MM_TM = 1664
MM_TN = 256
MM_TN_WIDE = 512


def kernel(x_prompt, x_sample, state_hgrn, state_conv, state_ssm, c_prompt, c_sample, lb_params, w_mod, b_mod,
           norm1_g, norm2_g, w_in, hg_norm_g, conv_w, conv_b, dt_bias, a_log, d_skip, ssm_norm_g, w_branch_a,
           w_branch_b, w_out, w_router_group, b_router_group, w_router_expert, b_router_expert, w_gate_e, w_up_e,
           w_down_e, final_g):
    nb, seq, d = x_prompt.shape
    ns = x_sample.shape[0]
    assert w_mod.shape[0] == 1 and x_sample.shape[1] == 1
    n_p = nb * seq
    n = n_p + ns
    nheads = state_hgrn.shape[2]
    hgw = nheads * HG_DK
    n_ssm_heads = state_ssm.shape[2]
    inner = n_ssm_heads * SSM_HEADDIM
    ng = n_ssm_heads // SSM_HPG
    cdim = inner + 2 * ng * SSM_STATE
    col_q, col_f, col_v, col_og = 0, hgw, 2 * hgw, 2 * hgw + d
    col_z = 2 * hgw + 2 * d
    col_x = col_z + inner
    n_main = col_x + cdim
    col_gab = n_main + n_ssm_heads
    row = lambda a: a.reshape(1, -1)

    lb = jnp.cumsum(jax.nn.softmax(lb_params.astype(F32), axis=0), axis=0)[0]

    n_c = ns + nb
    n_c_pad = -(-n_c // 8) * 8
    c_all = jnp.concatenate([c_sample, c_prompt, jnp.zeros((n_c_pad - n_c, d), F32)], axis=0)
    mod = _mm(c_all, w_mod[0], tm=n_c_pad, tn=512, name="mod", a_fn=_silu,
              extras=[(row(b_mod[0]), pl.BlockSpec((1, 512), lambda i, j: (0, j)))],
              epilogue=lambda acc, b: acc + b)
    p_mod = dict(nb=nb, per_row=False, mod_row0=ns)
    s_mod = dict(nb=1, nt=1, tr=ns, per_row=True, mod_row0=0)

    xp2 = x_prompt.reshape(n_p, d)
    xs2 = x_sample.reshape(ns, d)
    h = _norm_mod(xp2, mod, row(norm1_g[0]), nt=seq // 256, tr=256, sc_blk=1, sh_blk=0, into=None, total_rows=n,
                  out_row0=0, name="norm1_prompt", **p_mod)
    h = _norm_mod(xs2, mod, row(norm1_g[0]), sc_blk=1, sh_blk=0, into=h, total_rows=n, out_row0=n_p,
                  name="norm1_sample", **s_mod)

    assert n_ssm_heads == LANES
    u = _mm(h, w_in[0], tm=MM_TM, tn=MM_TN_WIDE, ncols=n_main, name="in_proj")
    u_t = _mm(h, w_in[0], tm=MM_TM, tn=MM_TN_WIDE, col0=n_main, ncols=2 * d, name="in_proj_tail")
    u_l = _mm(h, w_in[0], tm=MM_TM, tn=LANES, col0=n_main + 2 * d, ncols=LANES, name="in_proj_last")
    u_dt = u_t[:, :LANES]

    gn_h = row(hg_norm_g[0])
    o_a, hg_p = _hgrn_prompt(u, row(lb), gn_h, nb=nb, seq=seq, nheads=nheads, col_q=col_q, col_f=col_f,
                             col_v=col_v, col_og=col_og, total_rows=n, tb=512)
    f_t = u[n_p:, col_f:col_f + hgw].reshape(ns // STEP_TOKENS, STEP_TOKENS, hgw).transpose(0, 2, 1)
    o_a, hg_s = _hgrn_step(u, f_t, lb.reshape(hgw, 1), gn_h, state_hgrn[0], row0=n_p, nheads=nheads,
                           col_q=col_q, col_v=col_v, col_og=col_og, into=o_a)

    gn_s = row(ssm_norm_g[0])
    y_b, ssm_p = _ssd_prompt(u, u_dt, conv_w[0], row(conv_b[0]), row(dt_bias[0]), row(a_log[0]), row(d_skip[0]),
                             gn_s, nb=nb, seq=seq, col_z=col_z, col_x=col_x, total_rows=n, tb=256)
    xs, bm, cm, xdt, da = _ssd_prep(u, u_dt, state_conv[0].reshape(ns, (SSM_CONV - 1) * cdim), conv_w[0],
                                    row(conv_b[0]), row(dt_bias[0]), row(a_log[0]), row0=n_p, n=ns, col_x=col_x)
    cols = lambda a: a.reshape(ns // STEP_TOKENS, STEP_TOKENS, inner).transpose(0, 2, 1)
    dsk_full = jnp.repeat(d_skip[0], SSM_HEADDIM).reshape(1, inner)
    y_b, ssm_s = _ssd_step(u, xs, bm, cm, cols(xdt), cols(da), dsk_full, gn_s, state_ssm[0], row0=n_p,
                           col_z=col_z, into=y_b)
    xbc_rows = lambda r0, r1: lax.slice(u, (r0, col_x), (r1, col_x + cdim))
    conv_p = jnp.stack([xbc_rows((b + 1) * seq - (SSM_CONV - 1), (b + 1) * seq) for b in range(nb)])
    conv_s = jnp.concatenate([state_conv[0][:, 1:], xbc_rows(n_p, n)[:, None, :]], axis=1)

    tile_spec = lambda off: pl.BlockSpec((MM_TM, MM_TN), lambda i, j: (i, j + off // MM_TN))
    last_blk = u_t.shape[1] // LANES - 1
    half_spec = lambda blk0: pl.BlockSpec((MM_TM, LANES), lambda i, j: (i, jnp.minimum(blk0 + 2 * j, last_blk)))
    n_ct = d // MM_TN

    def gate_a(acc, g0, g1):
        return _sigmoid(jnp.concatenate([g0, g1], axis=-1)) * acc

    def gate_b(acc, g0, g1, gl, ta, pb):
        g1 = jnp.where(pl.program_id(1) == n_ct - 1, gl, g1)
        return ta + _sigmoid(jnp.concatenate([g0, g1], axis=-1)) * (pb + acc)

    t_a = _mm(o_a, w_branch_a[0], tm=MM_TM, tn=MM_TN, name="branch_a",
              extras=[(u_t, half_spec(1)), (u_t, half_spec(2))], epilogue=gate_a)
    half = inner // 2
    p_b = _mm(y_b, w_branch_b[0], tm=MM_TM, tn=MM_TN_WIDE, k=half, name="branch_b_lo")
    gb0 = 1 + d // LANES
    merged = _mm(y_b, w_branch_b[0], tm=MM_TM, tn=MM_TN, k=half, a_kblk=1, b_kblk=1, name="branch_b_hi",
                 extras=[(u_t, half_spec(gb0)), (u_t, half_spec(gb0 + 1)),
                         (u_l, pl.BlockSpec((MM_TM, LANES), lambda i, j: (i, 0))),
                         (t_a, tile_spec(0)), (p_b, tile_spec(0))],
                 epilogue=gate_b, out_dtype=BF16)
    proj = _mm(merged, w_out[0], tm=MM_TM, tn=MM_TN_WIDE, name="out_proj")

    wr = jnp.concatenate([w_router_group[0], w_router_expert[0],
                          jnp.zeros((d, LANES - N_GROUPS - N_EXPERTS), F32)], axis=1)
    wr_hi, wr_lo, _ = _split3(wr)
    wr3 = jnp.concatenate([wr_hi, wr_lo, wr_hi], axis=0)
    br = jnp.concatenate([b_router_group[0], b_router_expert[0],
                          jnp.zeros((LANES - N_GROUPS - N_EXPERTS,), F32)]).reshape(1, LANES)
    g2n = row(norm2_g[0])
    x1, h2, route = _resid_norm_router(xp2, proj, mod, g2n, wr3, br, nt=seq // 128, tr=128, gate_blk=2, sc_blk=4,
                                       sh_blk=3, proj_row0=0, intos=[None] * 3, total_rows=n,
                                       name="norm2_prompt", **p_mod)
    x1, h2, route = _resid_norm_router(xs2, proj, mod, g2n, wr3, br, gate_blk=2, sc_blk=4, sh_blk=3,
                                       proj_row0=n_p, intos=[x1, h2, route], total_rows=n,
                                       name="norm2_sample", **s_mod)

    e1 = route[:, 2].astype(jnp.int32)
    e2 = route[:, 3].astype(jnp.int32)
    n_tiles = -(-(2 * n + N_EXPERTS * (MOE_TILE - 1)) // MOE_TILE)
    pos, src, steps = _moe_plan(e1, e2, n_tiles)
    y_sorted = _moe_experts(h2, src, w_gate_e[0], w_up_e[0], w_down_e[0], steps, n_tiles)

    fg = row(final_g)
    y_prompt = _final(x1, y_sorted, pos, route, mod, fg, nt=seq // 128, tr=128, gate_blk=5, row0=0, out_rows=n_p,
                      name="final_prompt", **p_mod)
    y_sample = _final(x1, y_sorted, pos, route, mod, fg, gate_blk=5, row0=n_p, out_rows=ns, name="final_sample",
                      **s_mod)

    return (y_prompt.reshape(nb, seq, d), y_sample.reshape(ns, 1, d),
            hg_p[None], conv_p[None], ssm_p.reshape(1, nb, n_ssm_heads, SSM_HEADDIM, SSM_STATE),
            hg_s[None], conv_s[None], ssm_s[None])
```

```python
import functools

import numpy as np
import jax
import jax.numpy as jnp
from jax import lax
from jax.experimental import pallas as pl
from jax.experimental.pallas import tpu as pltpu

F32 = jnp.float32
BF16 = jnp.bfloat16
EPS = 1e-6

VMEM_LIMIT_BYTES = 56 * 1024 * 1024
LANES = 128

HG_DK = 128
HG_CHUNK = 128
HG_HEADS_PER_STEP = 4
SSM_HEADDIM = 64
SSM_STATE = 128
SSM_HPG = 16
SSM_CHUNK = 128
SSM_CONV = 4
N_GROUPS = 4
EXPERTS_PER_GROUP = 4
N_EXPERTS = 16
MOE_TILE = 128
MOE_UP_VMEM_BYTES = 60 * 1024 * 1024
SLAB_PITCH = 40


def _params(sem, vmem=VMEM_LIMIT_BYTES):
    return pltpu.CompilerParams(dimension_semantics=sem, vmem_limit_bytes=vmem)


def _sigmoid(x):
    return 1.0 / (1.0 + jnp.exp(-x))


def _silu(x):
    return x * _sigmoid(x)


def _softplus(x):
    return jnp.maximum(x, 0.0) + jnp.log(1.0 + jnp.exp(-jnp.abs(x)))


def _split3(x):
    p1 = x.astype(BF16)
    r1 = x - p1.astype(F32)
    p2 = r1.astype(BF16)
    r2 = r1 - p2.astype(F32)
    return p1, p2, r2.astype(BF16)


def _dot(a, b):
    return jnp.dot(a, b, preferred_element_type=F32)


def _dot_nt(a, b):
    return lax.dot_general(a, b, (((1,), (1,)), ((), ())), preferred_element_type=F32)


def _dot_tn(a, b):
    return lax.dot_general(a, b, (((0,), (0,)), ((), ())), preferred_element_type=F32)


def _exact_dot(w3_bf16, x_f32):
    return _dot(w3_bf16, jnp.concatenate(_split3(x_f32), axis=0))


def _exact_dot_rhs(x_f32, w3_bf16):
    return _dot(jnp.concatenate(_split3(x_f32), axis=-1), w3_bf16)


def _thrice(w, axis):
    return jnp.concatenate([w, w, w], axis=axis)


def _pcall(body, *, grid, in_specs, args, outs, sem, name, scratch=(), num_prefetch=0):
    n_in = len(args) - num_prefetch
    into = [(i, o[3]) for i, o in enumerate(outs) if o[3] is not None]
    in_specs = list(in_specs) + [pl.BlockSpec(memory_space=pl.ANY)] * len(into)
    all_args = list(args) + [buf for _, buf in into]
    aliases = {len(args) + j: i for j, (i, _) in enumerate(into)}

    def wrapped(*refs):
        keep = refs[:num_prefetch + n_in] + refs[num_prefetch + n_in + len(into):]
        return body(*keep)

    grid_spec = pltpu.PrefetchScalarGridSpec(
        num_scalar_prefetch=num_prefetch, grid=grid, in_specs=in_specs,
        out_specs=[o[2] for o in outs], scratch_shapes=list(scratch))
    res = pl.pallas_call(
        wrapped, grid_spec=grid_spec,
        out_shape=[jax.ShapeDtypeStruct(o[0], o[1]) for o in outs],
        input_output_aliases=aliases, compiler_params=_params(sem), name=name,
    )(*all_args)
    return res


def _mm_body(a_ref, b_ref, *rest, n_extra, a_fn, epilogue):
    extras = rest[:n_extra]
    o_ref = rest[n_extra]
    a = a_ref[...]
    if a_fn is not None:
        a = a_fn(a)
    acc = _dot(a.astype(BF16), b_ref[...].astype(BF16))
    if epilogue is not None:
        acc = epilogue(acc, *[e[...] for e in extras])
    o_ref[...] = acc.astype(o_ref.dtype)


def _mm(a, b, *, tm, tn, name, k=None, a_kblk=0, b_kblk=0, col0=0, ncols=None, extras=(), a_fn=None,
        epilogue=None, out_dtype=F32):
    m = a.shape[0]
    k = a.shape[1] if k is None else k
    ncols = b.shape[1] if ncols is None else ncols
    assert m % tm == 0 and ncols % tn == 0 and col0 % tn == 0
    cb = col0 // tn
    in_specs = [pl.BlockSpec((tm, k), lambda i, j: (i, a_kblk), pipeline_mode=pl.Buffered(1)),
                pl.BlockSpec((k, tn), lambda i, j: (b_kblk, j + cb))]
    in_specs += [s for _, s in extras]
    return pl.pallas_call(
        functools.partial(_mm_body, n_extra=len(extras), a_fn=a_fn, epilogue=epilogue),
        grid=(m // tm, ncols // tn),
        in_specs=in_specs,
        out_specs=pl.BlockSpec((tm, tn), lambda i, j: (i, j)),
        out_shape=jax.ShapeDtypeStruct((m, ncols), out_dtype),
        compiler_params=_params(("parallel", "arbitrary")),
        name=name,
    )(a, b, *[x for x, _ in extras])


def _rms(x, g):
    return x * lax.rsqrt(jnp.mean(x * x, axis=-1, keepdims=True) + EPS) * g


def _mod_view(mod, per_row):
    return mod if per_row else mod.reshape(mod.shape[0], 1, mod.shape[1])


def _mod_spec(per_row, tr, d, col_blk, nt, row0):
    if per_row:
        rb = row0 // tr
        return pl.BlockSpec((tr, d), lambda b, t: (rb + t, col_blk))
    return pl.BlockSpec((1, 1, d), lambda b, t: (row0 + b, 0, col_blk))


def _rows_spec(tr, d, nt, row0):
    rb = row0 // tr
    return pl.BlockSpec((tr, d), lambda b, t: (rb + b * nt + t, 0))


def _norm_mod_body(x_ref, sc_ref, sh_ref, g_ref, o_ref):
    d = x_ref.shape[-1]
    sc = sc_ref[...].reshape(-1, d)
    sh = sh_ref[...].reshape(-1, d)
    o_ref[...] = (_rms(x_ref[...], g_ref[...]) * (1.0 + sc) + sh).astype(o_ref.dtype)


def _norm_mod(x, mod, g, *, nb, nt, tr, per_row, mod_row0, sc_blk, sh_blk, into, total_rows, out_row0, name):
    d = x.shape[-1]
    mv = _mod_view(mod, per_row)
    return _pcall(
        _norm_mod_body, grid=(nb, nt),
        in_specs=[_rows_spec(tr, d, nt, 0),
                  _mod_spec(per_row, tr, d, sc_blk, nt, mod_row0),
                  _mod_spec(per_row, tr, d, sh_blk, nt, mod_row0),
                  pl.BlockSpec((1, d), lambda b, t: (0, 0))],
        args=[x, mv, mv, g],
        outs=[((total_rows, d), BF16, _rows_spec(tr, d, nt, out_row0), into)],
        sem=("parallel", "arbitrary"), name=name)[0]


def _router_math(lg):
    lane = lax.broadcasted_iota(jnp.int32, lg.shape, 1)
    neg = jnp.float32(-jnp.inf)
    gm = jnp.where(lane < N_GROUPS, lg, neg)
    gmax = jnp.max(gm, axis=-1, keepdims=True)
    gidx = jnp.min(jnp.where(gm == gmax, lane, LANES), axis=-1, keepdims=True)
    gsum = jnp.sum(jnp.where(lane < N_GROUPS, jnp.exp(gm - gmax), 0.0), axis=-1, keepdims=True)
    p_group = 1.0 / gsum
    lo = N_GROUPS + EXPERTS_PER_GROUP * gidx
    em = jnp.where((lane >= lo) & (lane < lo + EXPERTS_PER_GROUP), lg, neg)
    m1 = jnp.max(em, axis=-1, keepdims=True)
    i1 = jnp.min(jnp.where(em == m1, lane, LANES), axis=-1, keepdims=True)
    em2 = jnp.where(lane == i1, neg, em)
    m2 = jnp.max(em2, axis=-1, keepdims=True)
    i2 = jnp.min(jnp.where(em2 == m2, lane, LANES), axis=-1, keepdims=True)
    r = jnp.exp(m2 - m1)
    w1 = p_group / (1.0 + r)
    w2 = p_group * r / (1.0 + r)
    e1 = (i1 - N_GROUPS).astype(F32)
    e2 = (i2 - N_GROUPS).astype(F32)
    out = jnp.where(lane == 0, w1, 0.0)
    out = jnp.where(lane == 1, w2, out)
    out = jnp.where(lane == 2, e1, out)
    out = jnp.where(lane == 3, e2, out)
    return out


def _resid_norm_router_body(x_ref, p_ref, gt_ref, sc_ref, sh_ref, g_ref, wr_ref, br_ref, x1_ref, h_ref, rt_ref):
    d = x_ref.shape[-1]
    gt = gt_ref[...].reshape(-1, d)
    sc = sc_ref[...].reshape(-1, d)
    sh = sh_ref[...].reshape(-1, d)
    x1 = x_ref[...] + gt * p_ref[...]
    x1_ref[...] = x1
    h = _rms(x1, g_ref[...]) * (1.0 + sc) + sh
    tr = x_ref.shape[0]
    for k in range(SLAB_PITCH):
        slab = h[:, k * LANES:(k + 1) * LANES] if k < d // LANES else jnp.zeros((tr, LANES), F32)
        h_ref[pl.ds(k, tr, stride=SLAB_PITCH), :] = slab
    h1, h2, _ = _split3(h)
    lg = _dot(jnp.concatenate([h1, h1, h2], axis=-1), wr_ref[...])
    rt_ref[...] = _router_math(lg + br_ref[...])


def _resid_norm_router(x, proj, mod, g, wr3, br, *, nb, nt, tr, per_row, mod_row0, gate_blk, sc_blk, sh_blk,
                       proj_row0, intos, total_rows, name):
    d = x.shape[-1]
    mv = _mod_view(mod, per_row)
    ms = lambda blk: _mod_spec(per_row, tr, d, blk, nt, mod_row0)
    orow = _rows_spec(tr, d, nt, proj_row0)
    return _pcall(
        _resid_norm_router_body, grid=(nb, nt),
        in_specs=[_rows_spec(tr, d, nt, 0), orow, ms(gate_blk), ms(sc_blk), ms(sh_blk),
                  pl.BlockSpec((1, d), lambda b, t: (0, 0)),
                  pl.BlockSpec((3 * d, LANES), lambda b, t: (0, 0)),
                  pl.BlockSpec((1, LANES), lambda b, t: (0, 0))],
        args=[x, proj, mv, mv, mv, g, wr3, br],
        outs=[((total_rows, d), F32, orow, intos[0]),
              ((total_rows * SLAB_PITCH, LANES), F32, _rows_spec(tr * SLAB_PITCH, LANES, nt, proj_row0 * SLAB_PITCH),
               intos[1]),
              ((total_rows, LANES), F32, _rows_spec(tr, LANES, nt, proj_row0), intos[2])],
        sem=("parallel", "arbitrary"), name=name)


def _row_gather_start(src_hbm, idx_ref, idx0, dst, sem, n, unrolled=False, rows=1, pitch=1):
    def body(r, carry):
        s0 = idx_ref[idx0 + r] * pitch
        d0 = r * pitch
        if pitch % 8 == 0:
            s0 = pl.multiple_of(s0, 8)
            d0 = d0 if isinstance(d0, int) else pl.multiple_of(d0, 8)
        pltpu.make_async_copy(src_hbm.at[pl.ds(s0, rows), :], dst.at[pl.ds(d0, rows), :], sem).start()
        return carry

    if unrolled:
        for r in range(n):
            body(r, 0)
    else:
        lax.fori_loop(0, n, body, 0, unroll=8)


def _row_gather_wait(src_hbm, dst, sem, n, rows=1, pitch=1):
    def body(r, carry):
        d0 = pl.multiple_of(r * pitch, 8) if pitch % 8 == 0 else r * pitch
        pltpu.make_async_copy(src_hbm.at[pl.ds(0, rows), :], dst.at[pl.ds(d0, rows), :], sem).wait()
        return carry

    lax.fori_loop(0, n, body, 0, unroll=8)


def _final_body(pos_ref, x1_ref, rt_ref, gt_ref, g_ref, y_hbm, o_ref, ybuf, sem, *, tr, tok0, n_tok):
    d = x1_ref.shape[-1]
    i = pl.program_id(0)
    slot = lax.rem(i, 2)

    def start(step, sl):
        for k in range(2):
            _row_gather_start(y_hbm, pos_ref, k * n_tok + tok0 + step * tr, ybuf.at[sl, k], sem.at[sl], tr)

    @pl.when(i == 0)
    def _():
        start(0, 0)

    @pl.when(i + 1 < pl.num_programs(0))
    def _():
        start(i + 1, 1 - slot)

    for k in range(2):
        _row_gather_wait(y_hbm, ybuf.at[slot, k], sem.at[slot], tr)
    gt = gt_ref[...].reshape(-1, d)
    rt = rt_ref[...]
    moe = rt[:, 0:1] * ybuf[slot, 0] + rt[:, 1:2] * ybuf[slot, 1]
    o_ref[...] = _rms(x1_ref[...] + gt * moe, g_ref[...])


def _final(x1, y_sorted, pos, route, mod, g, *, nb, nt, tr, per_row, mod_row0, gate_blk, row0, out_rows, name):
    d = x1.shape[-1]
    n_tok = x1.shape[0]
    rb = row0 // tr
    if per_row:
        mspec = pl.BlockSpec((tr, d), lambda i, pos_: (mod_row0 // tr + i, gate_blk))
    else:
        mspec = pl.BlockSpec((1, 1, d), lambda i, pos_: (mod_row0 + i // nt, 0, gate_blk))
    return _pcall(
        functools.partial(_final_body, tr=tr, tok0=row0, n_tok=n_tok), grid=(nb * nt,),
        in_specs=[pl.BlockSpec((tr, d), lambda i, pos_: (rb + i, 0)),
                  pl.BlockSpec((tr, LANES), lambda i, pos_: (rb + i, 0)),
                  mspec,
                  pl.BlockSpec((1, d), lambda i, pos_: (0, 0)),
                  pl.BlockSpec(memory_space=pl.ANY)],
        args=[pos, x1, route, _mod_view(mod, per_row), g, y_sorted],
        outs=[((out_rows, d), F32, pl.BlockSpec((tr, d), lambda i, pos_: (i, 0)), None)],
        sem=("arbitrary",), name=name, num_prefetch=1,
        scratch=[pltpu.VMEM((2, 2, tr, d), F32), pltpu.SemaphoreType.DMA((2,))])[0]


HG_MXU_LEVELS = 3


def _hgrn_tables(c):
    nlev = int(np.log2(c))
    t = np.arange(c)[:, None]
    u = np.arange(c)[None, :]
    mats = [u <= t]
    masks = []
    for l in range(nlev):
        m = 1 << l
        p = (t // (2 * m)) * (2 * m) + m - 1
        upper = ((t >> l) & 1) == 1
        if l < HG_MXU_LEVELS:
            mats.append(np.where(upper, (u > p) & (u <= t), (u > t) & (u <= p)))
        same = (t >> (l + 1)) == (u >> (l + 1))
        masks.append(same & upper & (((u >> l) & 1) == 0))
    masks.append(t == u)
    w = np.concatenate([m.astype(np.float32) for m in mats], axis=0)
    return jnp.asarray(np.tile(w, (1, 3)), BF16), jnp.asarray(np.stack(masks).astype(np.float32))


def _hgrn_prompt_body(q_ref, f_ref, v_ref, og_ref, lb_ref, gn_ref, w_ref, mk_ref, o_ref, s_ref, st_scr, *,
                      nchunks, nlev):
    c = HG_CHUNK
    hp = st_scr.shape[0]
    t = pl.program_id(2)

    @pl.when(t == 0)
    def _():
        st_scr[...] = jnp.zeros_like(st_scr)

    lb = lb_ref[...]
    gn = gn_ref[...]
    trow = lax.broadcasted_iota(jnp.int32, (c, HG_DK), 0)
    upper = [((trow >> l) & 1) == 1 for l in range(HG_MXU_LEVELS, nlev)]

    def chunk(ci, carry):
        r0 = pl.multiple_of(ci * c, c)
        rows = pl.ds(r0, c)
        for i in range(hp):
            ln = slice(i * HG_DK, (i + 1) * HG_DK)
            f = lb[:, ln] + (1.0 - lb[:, ln]) * _sigmoid(f_ref[rows, ln])
            g = _exact_dot(w_ref[...], jnp.log2(f))
            kk = 1.0 - f
            q = _silu(q_ref[rows, ln])
            v = v_ref[rows, ln].astype(BF16)
            b = g[0:c]
            b_end = b[c - 1:c, :]
            st = st_scr[i]
            o = _dot_nt((q * jnp.exp2(b)).astype(BF16), st.astype(BF16))
            scores = mk_ref[nlev] * _dot_nt(q.astype(BF16), kk.astype(BF16))
            for l in range(nlev):
                if l < HG_MXU_LEVELS:
                    x = jnp.exp2(g[(1 + l) * c:(2 + l) * c])
                else:
                    m = 1 << l
                    b3 = b.reshape(c // (2 * m), 2 * m, HG_DK)
                    d = (b3 - b3[:, m - 1:m, :]).reshape(c, HG_DK)
                    x = jnp.exp2(jnp.where(upper[l - HG_MXU_LEVELS], d, -d))
                scores = scores + mk_ref[l] * _dot_nt((q * x).astype(BF16), (kk * x).astype(BF16))
            o = o + _dot(scores.astype(BF16), v)
            kw = (kk * jnp.exp2(b_end - b)).astype(BF16)
            st_scr[i] = st * jnp.exp2(b_end) + _dot_tn(v, kw)
            o = _rms(o, gn[:, ln]) * _silu(og_ref[rows, ln])
            o_ref[rows, ln] = o.astype(o_ref.dtype)
        return carry

    lax.fori_loop(0, nchunks, chunk, 0)

    @pl.when(t == pl.num_programs(2) - 1)
    def _():
        for i in range(hp):
            s_ref[0, i] = st_scr[i].T


def _hgrn_prompt(u, lb, gn, *, nb, seq, nheads, col_q, col_f, col_v, col_og, total_rows, tb):
    c = HG_CHUNK
    nt = seq // tb
    nlev = int(np.log2(c))
    w, mk = _hgrn_tables(c)
    hp = HG_HEADS_PER_STEP
    wd = hp * HG_DK
    cs = lambda col: pl.BlockSpec((tb, wd), lambda b, h, t: (b * nt + t, col // wd + h))
    hv = pl.BlockSpec((1, wd), lambda b, h, t: (0, h))
    return _pcall(
        functools.partial(_hgrn_prompt_body, nchunks=tb // c, nlev=nlev),
        grid=(nb, nheads // hp, nt),
        in_specs=[cs(col_q), cs(col_f), cs(col_v), cs(col_og), hv, hv,
                  pl.BlockSpec(w.shape, lambda b, h, t: (0, 0)),
                  pl.BlockSpec(mk.shape, lambda b, h, t: (0, 0, 0))],
        args=[u, u, u, u, lb, gn, w, mk],
        outs=[((total_rows, nheads * HG_DK), BF16, pl.BlockSpec((tb, wd), lambda b, h, t: (b * nt + t, h)), None),
              ((nb, nheads, HG_DK, HG_DK), F32, pl.BlockSpec((1, hp, HG_DK, HG_DK), lambda b, h, t: (b, h, 0, 0)), None)],
        sem=("parallel", "parallel", "arbitrary"), name="hgrn_prompt",
        scratch=[pltpu.VMEM((hp, HG_DK, HG_DK), F32)])


def _ssm_expand_tables(n_heads):
    ng = n_heads // SSM_HPG
    w2, w1 = SSM_HPG * LANES, SSM_HPG * SSM_HEADDIM
    e = np.zeros((ng, n_heads, w2 + w1), np.float32)
    for g in range(ng):
        for i in range(SSM_HPG):
            e[g, g * SSM_HPG + i, i * LANES:(i + 1) * LANES] = 1.0
            e[g, g * SSM_HPG + i, w2 + i * SSM_HEADDIM:w2 + (i + 1) * SSM_HEADDIM] = 1.0
    return jnp.asarray(np.tile(e, (1, 3, 1)), BF16)


def _conv_silu(buf_ref, w, bias, r0, n):
    b = buf_ref[pl.ds(r0, n + 8), :]
    acc = b * w[0:1, :]
    for j in range(1, SSM_CONV):
        acc = pltpu.roll(acc, 1, axis=0) + b * w[j:j + 1, :]
    return _silu(acc[8:, :] + bias)


def _ssd_prompt_body(x_ref, b_ref, c_ref, z_ref, dt_ref, wx_ref, wb_ref, wc_ref, bx_ref, bb_ref, bc_ref,
                     dtb_ref, alog_ref, dsk_ref, e_ref, gn_ref, y_ref, s_ref,
                     ht_scr, cbx, cbb, cbc, *, nchunks):
    L = SSM_CHUNK
    tb = nchunks * L
    w2 = SSM_HPG * LANES
    t = pl.program_id(2)

    @pl.when(t == 0)
    def _():
        ht_scr[...] = jnp.zeros_like(ht_scr)
        cbx[0:8, :] = jnp.zeros((8, cbx.shape[1]), F32)
        cbb[0:8, :] = jnp.zeros((8, cbb.shape[1]), F32)
        cbc[0:8, :] = jnp.zeros((8, cbc.shape[1]), F32)

    cbx[8:8 + tb, :] = x_ref[...]
    cbb[8:8 + tb, :] = b_ref[...]
    cbc[8:8 + tb, :] = c_ref[...]

    e_all = e_ref[0]
    e1 = e_all[:, w2:]
    neg_a = -jnp.exp(alog_ref[...])
    dsk1 = _exact_dot_rhs(dsk_ref[...], e1)
    row = lax.broadcasted_iota(jnp.int32, (L, L), 0)
    col = lax.broadcasted_iota(jnp.int32, (L, L), 1)
    causal = col <= row
    tril = _thrice(jnp.where(causal, 1.0, 0.0).astype(BF16), 1)
    lane = lax.broadcasted_iota(jnp.int32, (L, LANES), 1)
    lo_half = lane < SSM_HEADDIM

    for ci in range(nchunks):
        r0 = ci * L
        xs = _conv_silu(cbx, wx_ref[...], bx_ref[...], r0, L)
        bm = _conv_silu(cbb, wb_ref[...], bb_ref[...], r0, L).astype(BF16)
        cm = _conv_silu(cbc, wc_ref[...], bc_ref[...], r0, L).astype(BF16)
        dt = _softplus(dt_ref[r0:r0 + L, :] + dtb_ref[...])
        cum = _exact_dot(tril, dt * neg_a)
        cum_e = _exact_dot_rhs(cum, e_all)
        cum1 = cum_e[:, w2:]
        dt1 = _exact_dot_rhs(dt, e1)
        xdt = xs * dt1
        cum_end = cum1[L - 1:L, :]
        ht = ht_scr[...]
        y = _dot(cm, ht.astype(BF16)) * jnp.exp(cum1)
        cb = _dot_nt(cm, bm)
        parts = []
        for j in range(SSM_HPG // 2):
            xp = xdt[:, j * LANES:(j + 1) * LANES]
            acc = None
            for half in range(2):
                i = 2 * j + half
                colb = cum_e[:, i * LANES:(i + 1) * LANES]
                seg = colb - colb.T
                m = (cb * jnp.exp(jnp.where(causal, seg, -jnp.inf))).astype(BF16)
                xh = jnp.where(lo_half if half == 0 else ~lo_half, xp, 0.0).astype(BF16)
                d = _dot(m, xh)
                acc = d if acc is None else acc + d
            parts.append(acc)
        y = y + jnp.concatenate(parts, axis=-1) + dsk1 * xs
        y = y * _silu(z_ref[r0:r0 + L, :])
        y_ref[r0:r0 + L, :] = _rms(y, gn_ref[...]).astype(y_ref.dtype)
        xw = (xdt * jnp.exp(cum_end - cum1)).astype(BF16)
        ht_scr[...] = ht * jnp.exp(cum_end) + _dot_tn(bm, xw)

    cbx[0:8, :] = cbx[tb:tb + 8, :]
    cbb[0:8, :] = cbb[tb:tb + 8, :]
    cbc[0:8, :] = cbc[tb:tb + 8, :]

    @pl.when(t == pl.num_programs(2) - 1)
    def _():
        s_ref[0, 0] = ht_scr[...].T


def _ssd_prompt(u, u_dt, conv_w, conv_b, dt_bias, a_log, d_skip, gn, *, nb, seq, col_z, col_x, total_rows, tb):
    n_heads = u_dt.shape[1]
    ng = n_heads // SSM_HPG
    w1 = SSM_HPG * SSM_HEADDIM
    inner = ng * w1
    nt = seq // tb
    e = _ssm_expand_tables(n_heads)
    xb, bb, cb_, zb = col_x // w1, (col_x + inner) // LANES, (col_x + inner + ng * SSM_STATE) // LANES, col_z // w1
    wide = lambda blk0: pl.BlockSpec((tb, w1), lambda b, g, t: (b * nt + t, blk0 + g))
    thin = lambda blk0: pl.BlockSpec((tb, LANES), lambda b, g, t: (b * nt + t, blk0 + g))
    cw = lambda rows, width, blk0: pl.BlockSpec((rows, width), lambda b, g, t: (0, blk0 + g))
    full = lambda a: pl.BlockSpec(a.shape, lambda b, g, t: (0,) * a.ndim)
    return _pcall(
        functools.partial(_ssd_prompt_body, nchunks=tb // SSM_CHUNK),
        grid=(nb, ng, nt),
        in_specs=[wide(xb), thin(bb), thin(cb_), wide(zb),
                  pl.BlockSpec((tb, n_heads), lambda b, g, t: (b * nt + t, 0)),
                  cw(SSM_CONV, w1, 0), cw(SSM_CONV, LANES, inner // LANES),
                  cw(SSM_CONV, LANES, (inner + ng * SSM_STATE) // LANES),
                  cw(1, w1, 0), cw(1, LANES, inner // LANES), cw(1, LANES, (inner + ng * SSM_STATE) // LANES),
                  full(dt_bias), full(a_log), full(d_skip),
                  pl.BlockSpec((1,) + e.shape[1:], lambda b, g, t: (g, 0, 0)),
                  cw(1, w1, 0)],
        args=[u, u, u, u, u_dt, conv_w, conv_w, conv_w, conv_b, conv_b, conv_b, dt_bias, a_log, d_skip, e, gn],
        outs=[((total_rows, inner), BF16, pl.BlockSpec((tb, w1), lambda b, g, t: (b * nt + t, g)), None),
              ((nb, ng, w1, SSM_STATE), F32, pl.BlockSpec((1, 1, w1, SSM_STATE), lambda b, g, t: (b, g, 0, 0)), None)],
        sem=("parallel", "parallel", "arbitrary"), name="ssd_prompt",
        scratch=[pltpu.VMEM((SSM_STATE, w1), F32), pltpu.VMEM((tb + 8, w1), F32),
                 pltpu.VMEM((tb + 8, LANES), F32), pltpu.VMEM((tb + 8, LANES), F32)])


STEP_TOKENS = 16
STEP_HEADS = 4


def _hgrn_step_body(q_ref, v_ref, og_ref, ft_ref, lbc_ref, gn_ref, s_ref, o_ref, so_ref):
    lbc = lbc_ref[...]
    f = lbc + (1.0 - lbc) * _sigmoid(ft_ref[0])
    kk = 1.0 - f
    q = _silu(q_ref[...])
    v = v_ref[...]
    gn = gn_ref[...]
    og = og_ref[...]
    for i in range(STEP_HEADS):
        ln = slice(i * HG_DK, (i + 1) * HG_DK)
        rows = []
        for j in range(STEP_TOKENS):
            sn = f[ln, j:j + 1] * s_ref[j, i] + kk[ln, j:j + 1] * v[j:j + 1, ln]
            so_ref[j, i] = sn
            qj = jnp.broadcast_to(q[j:j + 1, ln], (16, HG_DK)).astype(BF16)
            rows.append(_dot(qj, sn.astype(BF16))[0:1, :])
        o = jnp.concatenate(rows, axis=0)
        o_ref[:, ln] = (_rms(o, gn[:, ln]) * _silu(og[:, ln])).astype(o_ref.dtype)


def _hgrn_step(u, f_t, lb_col, gn, state, *, row0, nheads, col_q, col_v, col_og, into):
    n = state.shape[0]
    tb = STEP_TOKENS
    hp = STEP_HEADS
    wd = hp * HG_DK
    rb = row0 // tb
    cs = lambda col: pl.BlockSpec((tb, wd), lambda b, h: (rb + b, col // wd + h))
    sspec = pl.BlockSpec((tb, hp, HG_DK, HG_DK), lambda b, h: (b, h, 0, 0))
    return _pcall(
        _hgrn_step_body, grid=(n // tb, nheads // hp),
        in_specs=[cs(col_q), cs(col_v), cs(col_og),
                  pl.BlockSpec((1, wd, tb), lambda b, h: (b, h, 0)),
                  pl.BlockSpec((wd, 1), lambda b, h: (h, 0)),
                  pl.BlockSpec((1, wd), lambda b, h: (0, h)),
                  sspec],
        args=[u, u, u, f_t, lb_col, gn, state],
        outs=[(into.shape, into.dtype, pl.BlockSpec((tb, wd), lambda b, h: (rb + b, h)), into),
              (state.shape, F32, sspec, None)],
        sem=("parallel", "arbitrary"), name="hgrn_step")


def _ssd_prep_body(x_ref, b_ref, c_ref, sx0, sx1, sx2, sb0, sb1, sb2, sc0, sc1, sc2, dt_ref,
                   wx_ref, wb_ref, wc_ref, bx_ref, bb_ref, bc_ref, dtb_ref, alog_ref, e_ref,
                   xs_ref, bm_ref, cm_ref, xdt_ref, da_ref):
    def conv(taps, cur, w_ref, bias_ref):
        w = w_ref[...]
        acc = bias_ref[...] + cur[...] * w[SSM_CONV - 1:SSM_CONV, :]
        for j, tap in enumerate(taps):
            acc = acc + tap[...] * w[j:j + 1, :]
        return _silu(acc)

    xs = conv((sx0, sx1, sx2), x_ref, wx_ref, bx_ref)
    xs_ref[...] = xs
    bm_ref[...] = conv((sb0, sb1, sb2), b_ref, wb_ref, bb_ref)
    cm_ref[...] = conv((sc0, sc1, sc2), c_ref, wc_ref, bc_ref)
    e1 = e_ref[0][:, SSM_HPG * LANES:]
    dt = _softplus(dt_ref[...] + dtb_ref[...])
    xdt_ref[...] = xs * _exact_dot_rhs(dt, e1)
    da_ref[...] = _exact_dot_rhs(jnp.exp(dt * -jnp.exp(alog_ref[...])), e1)


def _ssd_prep(u, u_dt, conv_state2d, conv_w, conv_b, dt_bias, a_log, *, row0, n, col_x):
    n_heads = u_dt.shape[1]
    ng = n_heads // SSM_HPG
    w1 = SSM_HPG * SSM_HEADDIM
    inner = ng * w1
    cdim = inner + 2 * ng * SSM_STATE
    e = _ssm_expand_tables(n_heads)
    rb = row0 // n
    xoff, boff, coff = 0, inner, inner + ng * SSM_STATE
    cur = lambda width, off: pl.BlockSpec((n, width), lambda g: (rb, (col_x + off) // width + g))
    tap = lambda width, off, j: pl.BlockSpec((n, width), lambda g: (0, (j * cdim + off) // width + g))
    cw = lambda rows, width, off: pl.BlockSpec((rows, width), lambda g: (0, off // width + g))
    full = lambda a: pl.BlockSpec(a.shape, lambda g: (0,) * a.ndim)
    taps = [tap(w1, xoff, j) for j in range(3)] + [tap(LANES, boff, j) for j in range(3)] + \
           [tap(LANES, coff, j) for j in range(3)]
    o_wide = pl.BlockSpec((n, w1), lambda g: (0, g))
    o_thin = pl.BlockSpec((n, LANES), lambda g: (0, g))
    return _pcall(
        _ssd_prep_body, grid=(ng,),
        in_specs=[cur(w1, xoff), cur(LANES, boff), cur(LANES, coff)] + taps +
                 [pl.BlockSpec((n, n_heads), lambda g: (rb, 0)),
                  cw(SSM_CONV, w1, xoff), cw(SSM_CONV, LANES, boff), cw(SSM_CONV, LANES, coff),
                  cw(1, w1, xoff), cw(1, LANES, boff), cw(1, LANES, coff),
                  full(dt_bias), full(a_log),
                  pl.BlockSpec((1,) + e.shape[1:], lambda g: (g, 0, 0))],
        args=[u, u, u] + [conv_state2d] * 9 + [u_dt, conv_w, conv_w, conv_w, conv_b, conv_b, conv_b,
                                               dt_bias, a_log, e],
        outs=[((n, inner), F32, o_wide, None), ((n, ng * SSM_STATE), F32, o_thin, None),
              ((n, ng * SSM_STATE), F32, o_thin, None), ((n, inner), F32, o_wide, None),
              ((n, inner), F32, o_wide, None)],
        sem=("arbitrary",), name="ssd_prep")


def _ssd_step_body(xs_ref, bm_ref, cm_ref, z_ref, xdt_ref, da_ref, dsk_ref, gn_ref, s_ref, y_ref, so_ref):
    w1 = SSM_HPG * SSM_HEADDIM
    bm = bm_ref[...]
    cm = cm_ref[...]
    xdt = xdt_ref[0]
    da = da_ref[0]
    rows = []
    for j in range(STEP_TOKENS):
        h = s_ref[j].reshape(w1, SSM_STATE)
        hn = da[:, j:j + 1] * h + xdt[:, j:j + 1] * bm[j:j + 1, :]
        so_ref[j] = hn.reshape(SSM_HPG, SSM_HEADDIM, SSM_STATE)
        cj = jnp.broadcast_to(cm[j:j + 1, :], (16, SSM_STATE)).astype(BF16)
        rows.append(_dot_nt(cj, hn.astype(BF16))[0:1, :])
    y = jnp.concatenate(rows, axis=0) + dsk_ref[...] * xs_ref[...]
    y = y * _silu(z_ref[...])
    y_ref[...] = _rms(y, gn_ref[...]).astype(y_ref.dtype)


def _ssd_step(u, xs, bm, cm, xdt_t, da_t, dsk_full, gn, state, *, row0, col_z, into):
    n, n_heads = state.shape[0], state.shape[1]
    ng = n_heads // SSM_HPG
    w1 = SSM_HPG * SSM_HEADDIM
    tb = STEP_TOKENS
    rb = row0 // tb
    sspec = pl.BlockSpec((tb, SSM_HPG, SSM_HEADDIM, SSM_STATE), lambda b, g: (b, g, 0, 0))
    colspec = pl.BlockSpec((1, w1, tb), lambda b, g: (b, g, 0))
    return _pcall(
        _ssd_step_body, grid=(n // tb, ng),
        in_specs=[pl.BlockSpec((tb, w1), lambda b, g: (b, g)),
                  pl.BlockSpec((tb, SSM_STATE), lambda b, g: (b, g)),
                  pl.BlockSpec((tb, SSM_STATE), lambda b, g: (b, g)),
                  pl.BlockSpec((tb, w1), lambda b, g: (rb + b, col_z // w1 + g)),
                  colspec, colspec,
                  pl.BlockSpec((1, w1), lambda b, g: (0, g)),
                  pl.BlockSpec((1, w1), lambda b, g: (0, g)),
                  sspec],
        args=[xs, bm, cm, u, xdt_t, da_t, dsk_full, gn, state],
        outs=[(into.shape, into.dtype, pl.BlockSpec((tb, w1), lambda b, g: (rb + b, g)), into),
              (state.shape, F32, sspec, None)],
        sem=("parallel", "arbitrary"), name="ssd_step")


def _moe_plan(e1, e2, n_tiles):
    n = e1.shape[0]
    tile = MOE_TILE
    eid = jnp.concatenate([e1, e2]).astype(jnp.int32)
    onehot = (eid[:, None] == jnp.arange(N_EXPERTS, dtype=jnp.int32)[None, :]).astype(jnp.int32)
    blk = 128
    assert (2 * n) % blk == 0
    oh3 = onehot.astype(F32).reshape(2 * n // blk, blk, N_EXPERTS)
    tri = jnp.tril(jnp.ones((blk, blk), F32), -1)
    inblk = jnp.einsum("ts,bse->bte", tri, oh3, preferred_element_type=F32)
    btot = jnp.sum(oh3, axis=1)
    before = (jnp.cumsum(btot, axis=0) - btot)[:, None, :] + inblk
    rank = jnp.sum(before.reshape(2 * n, N_EXPERTS) * onehot, axis=1).astype(jnp.int32)
    counts = jnp.sum(onehot, axis=0)
    ptiles = (counts + tile - 1) // tile
    tstart = jnp.cumsum(ptiles) - ptiles
    pos = (tstart * tile)[eid] + rank
    tok = jnp.concatenate([jnp.arange(n, dtype=jnp.int32)] * 2)
    src = jnp.zeros((n_tiles * tile,), jnp.int32).at[pos].set(tok)
    n_used = jnp.sum(ptiles)
    s = jnp.arange(n_tiles, dtype=jnp.int32)
    tend = jnp.cumsum(ptiles)
    ex = jnp.searchsorted(tend, jnp.minimum(s, n_used - 1), side="right").astype(jnp.int32)
    active = (s < n_used).astype(jnp.int32)
    first = jnp.concatenate([jnp.ones((1,), jnp.int32), (ex[1:] != ex[:-1]).astype(jnp.int32)])
    nstep = tend[ex]
    has_next = (nstep < n_used).astype(jnp.int32)
    nex = jnp.where(has_next == 1, ex[jnp.minimum(nstep, n_tiles - 1)], ex)
    i32 = lambda a: a.astype(jnp.int32)
    return i32(pos), src, (i32(ex), i32(nex), has_next, first, active)


def _moe_up_body(ex_ref, nex_ref, hasn_ref, first_ref, act_ref, src_ref, x_hbm, wg_hbm, wu_hbm, o_ref,
                 stage_g, stage_u, wg_s, wu_s, xbuf, sem, wsem):
    s = pl.program_id(0)
    last = pl.num_programs(0) - 1
    slot = lax.rem(s, 2)
    tile = o_ref.shape[0]
    n_slab = wg_s.shape[0] // LANES
    slab = dict(rows=n_slab, pitch=SLAB_PITCH)
    nxt = jnp.minimum(s + 1, last)

    def start(step, sl, unrolled=False):
        _row_gather_start(x_hbm, src_ref, step * tile, xbuf.at[sl], sem.at[sl], tile, unrolled, **slab)

    def weight_copies(e):
        return (pltpu.make_async_copy(wg_hbm.at[e], stage_g, wsem.at[0]),
                pltpu.make_async_copy(wu_hbm.at[e], stage_u, wsem.at[1]))

    @pl.when(s == 0)
    def _():
        start(0, 0)
        for cp in weight_copies(ex_ref[0]):
            cp.start()

    @pl.when(first_ref[s] == 1)
    def _():
        for cp, stage, dst in zip(weight_copies(ex_ref[s]), (stage_g, stage_u), (wg_s, wu_s)):
            cp.wait()
            dst[...] = stage[...].astype(BF16)

        @pl.when(hasn_ref[s] == 1)
        def _():
            for cp in weight_copies(nex_ref[s]):
                cp.start()

    _row_gather_wait(x_hbm, xbuf.at[slot], sem.at[slot], tile, **slab)

    @pl.when(act_ref[s] == 1)
    def _():
        xb = xbuf.at[slot]
        x = jnp.concatenate([xb[pl.ds(k, tile, stride=SLAB_PITCH), :] for k in range(n_slab)], axis=-1)
        x = x.astype(BF16)
        start(nxt, 1 - slot, unrolled=True)
        o_ref[...] = (_silu(_dot(x, wg_s[...])) * _dot(x, wu_s[...])).astype(o_ref.dtype)

    @pl.when(act_ref[s] == 0)
    def _():
        start(nxt, 1 - slot)
        o_ref[...] = jnp.zeros_like(o_ref)

    @pl.when(s == last)
    def _():
        _row_gather_wait(x_hbm, xbuf.at[1 - slot], sem.at[1 - slot], tile, **slab)


def _moe_down_body(ex_ref, nex_ref, hasn_ref, first_ref, act_ref, a_ref, wd_ref, o_ref, wd_s):
    s = pl.program_id(0)

    @pl.when(first_ref[s] == 1)
    def _():
        wd_s[...] = wd_ref[0].astype(BF16)

    @pl.when(act_ref[s] == 1)
    def _():
        o_ref[...] = _dot(a_ref[...], wd_s[...])

    @pl.when(act_ref[s] == 0)
    def _():
        o_ref[...] = jnp.zeros_like(o_ref)


def _moe_experts(x, src, w_gate, w_up, w_down, steps, n_tiles):
    tile = MOE_TILE
    d, de = w_gate.shape[1], w_gate.shape[2]
    rows = n_tiles * tile
    any_spec = pl.BlockSpec(memory_space=pl.ANY)
    act = pl.pallas_call(
        _moe_up_body,
        grid_spec=pltpu.PrefetchScalarGridSpec(
            num_scalar_prefetch=6, grid=(n_tiles,),
            in_specs=[any_spec, any_spec, any_spec],
            out_specs=pl.BlockSpec((tile, de), lambda s, *_: (s, 0)),
            scratch_shapes=[pltpu.VMEM((d, de), F32), pltpu.VMEM((d, de), F32),
                            pltpu.VMEM((d, de), BF16), pltpu.VMEM((d, de), BF16),
                            pltpu.VMEM((2, tile * SLAB_PITCH, LANES), F32),
                            pltpu.SemaphoreType.DMA((2,)), pltpu.SemaphoreType.DMA((2,))]),
        out_shape=jax.ShapeDtypeStruct((rows, de), BF16),
        compiler_params=_params(("arbitrary",), vmem=MOE_UP_VMEM_BYTES), name="moe_up",
    )(*steps, src, x, w_gate, w_up)
    wmap = lambda s, ex, nex, hasn, fi, ac: (jnp.where(fi[s] == 1, ex[s], nex[s]), 0, 0)
    y = pl.pallas_call(
        _moe_down_body,
        grid_spec=pltpu.PrefetchScalarGridSpec(
            num_scalar_prefetch=5, grid=(n_tiles,),
            in_specs=[pl.BlockSpec((tile, de), lambda s, *_: (s, 0)), pl.BlockSpec((1, de, d), wmap)],
            out_specs=pl.BlockSpec((tile, d), lambda s, *_: (s, 0)),
            scratch_shapes=[pltpu.VMEM((de, d), BF16)]),
        out_shape=jax.ShapeDtypeStruct((rows, d), F32),
        compiler_params=_params(("arbitrary",)), name="moe_down",
    )(*steps, act, w_down)
    return y


MM_TM = 1664
MM_TN = 256
MM_TN_WIDE = 512


def kernel(x_prompt, x_sample, state_hgrn, state_conv, state_ssm, c_prompt, c_sample, lb_params, w_mod, b_mod,
           norm1_g, norm2_g, w_in, hg_norm_g, conv_w, conv_b, dt_bias, a_log, d_skip, ssm_norm_g, w_branch_a,
           w_branch_b, w_out, w_router_group, b_router_group, w_router_expert, b_router_expert, w_gate_e, w_up_e,
           w_down_e, final_g):
    nb, seq, d = x_prompt.shape
    ns = x_sample.shape[0]
    assert w_mod.shape[0] == 1 and x_sample.shape[1] == 1
    n_p = nb * seq
    n = n_p + ns
    nheads = state_hgrn.shape[2]
    hgw = nheads * HG_DK
    n_ssm_heads = state_ssm.shape[2]
    inner = n_ssm_heads * SSM_HEADDIM
    ng = n_ssm_heads // SSM_HPG
    cdim = inner + 2 * ng * SSM_STATE
    col_q, col_f, col_v, col_og = 0, hgw, 2 * hgw, 2 * hgw + d
    col_z = 2 * hgw + 2 * d
    col_x = col_z + inner
    n_main = col_x + cdim
    col_gab = n_main + n_ssm_heads
    row = lambda a: a.reshape(1, -1)

    lb = jnp.cumsum(jax.nn.softmax(lb_params.astype(F32), axis=0), axis=0)[0]

    n_c = ns + nb
    n_c_pad = -(-n_c // 8) * 8
    c_all = jnp.concatenate([c_sample, c_prompt, jnp.zeros((n_c_pad - n_c, d), F32)], axis=0)
    mod = _mm(c_all, w_mod[0], tm=n_c_pad, tn=512, name="mod", a_fn=_silu,
              extras=[(row(b_mod[0]), pl.BlockSpec((1, 512), lambda i, j: (0, j)))],
              epilogue=lambda acc, b: acc + b)
    p_mod = dict(nb=nb, per_row=False, mod_row0=ns)
    s_mod = dict(nb=1, nt=1, tr=ns, per_row=True, mod_row0=0)

    xp2 = x_prompt.reshape(n_p, d)
    xs2 = x_sample.reshape(ns, d)
    h = _norm_mod(xp2, mod, row(norm1_g[0]), nt=seq // 256, tr=256, sc_blk=1, sh_blk=0, into=None, total_rows=n,
                  out_row0=0, name="norm1_prompt", **p_mod)
    h = _norm_mod(xs2, mod, row(norm1_g[0]), sc_blk=1, sh_blk=0, into=h, total_rows=n, out_row0=n_p,
                  name="norm1_sample", **s_mod)

    assert n_ssm_heads == LANES
    u = _mm(h, w_in[0], tm=MM_TM, tn=MM_TN_WIDE, ncols=n_main, name="in_proj")
    u_t = _mm(h, w_in[0], tm=MM_TM, tn=MM_TN_WIDE, col0=n_main, ncols=2 * d, name="in_proj_tail")
    u_l = _mm(h, w_in[0], tm=MM_TM, tn=LANES, col0=n_main + 2 * d, ncols=LANES, name="in_proj_last")
    u_dt = u_t[:, :LANES]

    gn_h = row(hg_norm_g[0])
    o_a, hg_p = _hgrn_prompt(u, row(lb), gn_h, nb=nb, seq=seq, nheads=nheads, col_q=col_q, col_f=col_f,
                             col_v=col_v, col_og=col_og, total_rows=n, tb=512)
    f_t = u[n_p:, col_f:col_f + hgw].reshape(ns // STEP_TOKENS, STEP_TOKENS, hgw).transpose(0, 2, 1)
    o_a, hg_s = _hgrn_step(u, f_t, lb.reshape(hgw, 1), gn_h, state_hgrn[0], row0=n_p, nheads=nheads,
                           col_q=col_q, col_v=col_v, col_og=col_og, into=o_a)

    gn_s = row(ssm_norm_g[0])
    y_b, ssm_p = _ssd_prompt(u, u_dt, conv_w[0], row(conv_b[0]), row(dt_bias[0]), row(a_log[0]), row(d_skip[0]),
                             gn_s, nb=nb, seq=seq, col_z=col_z, col_x=col_x, total_rows=n, tb=256)
    xs, bm, cm, xdt, da = _ssd_prep(u, u_dt, state_conv[0].reshape(ns, (SSM_CONV - 1) * cdim), conv_w[0],
                                    row(conv_b[0]), row(dt_bias[0]), row(a_log[0]), row0=n_p, n=ns, col_x=col_x)
    cols = lambda a: a.reshape(ns // STEP_TOKENS, STEP_TOKENS, inner).transpose(0, 2, 1)
    dsk_full = jnp.repeat(d_skip[0], SSM_HEADDIM).reshape(1, inner)
    y_b, ssm_s = _ssd_step(u, xs, bm, cm, cols(xdt), cols(da), dsk_full, gn_s, state_ssm[0], row0=n_p,
                           col_z=col_z, into=y_b)
    xbc_rows = lambda r0, r1: lax.slice(u, (r0, col_x), (r1, col_x + cdim))
    conv_p = jnp.stack([xbc_rows((b + 1) * seq - (SSM_CONV - 1), (b + 1) * seq) for b in range(nb)])
    conv_s = jnp.concatenate([state_conv[0][:, 1:], xbc_rows(n_p, n)[:, None, :]], axis=1)

    tile_spec = lambda off: pl.BlockSpec((MM_TM, MM_TN), lambda i, j: (i, j + off // MM_TN))
    last_blk = u_t.shape[1] // LANES - 1
    half_spec = lambda blk0: pl.BlockSpec((MM_TM, LANES), lambda i, j: (i, jnp.minimum(blk0 + 2 * j, last_blk)))
    n_ct = d // MM_TN

    def gate_a(acc, g0, g1):
        return _sigmoid(jnp.concatenate([g0, g1], axis=-1)) * acc

    def gate_b(acc, g0, g1, gl, ta, pb):
        g1 = jnp.where(pl.program_id(1) == n_ct - 1, gl, g1)
        return ta + _sigmoid(jnp.concatenate([g0, g1], axis=-1)) * (pb + acc)

    t_a = _mm(o_a, w_branch_a[0], tm=MM_TM, tn=MM_TN, name="branch_a",
              extras=[(u_t, half_spec(1)), (u_t, half_spec(2))], epilogue=gate_a)
    half = inner // 2
    p_b = _mm(y_b, w_branch_b[0], tm=MM_TM, tn=MM_TN_WIDE, k=half, name="branch_b_lo")
    gb0 = 1 + d // LANES
    merged = _mm(y_b, w_branch_b[0], tm=MM_TM, tn=MM_TN, k=half, a_kblk=1, b_kblk=1, name="branch_b_hi",
                 extras=[(u_t, half_spec(gb0)), (u_t, half_spec(gb0 + 1)),
                         (u_l, pl.BlockSpec((MM_TM, LANES), lambda i, j: (i, 0))),
                         (t_a, tile_spec(0)), (p_b, tile_spec(0))],
                 epilogue=gate_b, out_dtype=BF16)
    proj = _mm(merged, w_out[0], tm=MM_TM, tn=MM_TN_WIDE, name="out_proj")

    wr = jnp.concatenate([w_router_group[0], w_router_expert[0],
                          jnp.zeros((d, LANES - N_GROUPS - N_EXPERTS), F32)], axis=1)
    wr_hi, wr_lo, _ = _split3(wr)
    wr3 = jnp.concatenate([wr_hi, wr_lo, wr_hi], axis=0)
    br = jnp.concatenate([b_router_group[0], b_router_expert[0],
                          jnp.zeros((LANES - N_GROUPS - N_EXPERTS,), F32)]).reshape(1, LANES)
    g2n = row(norm2_g[0])
    x1, h2, route = _resid_norm_router(xp2, proj, mod, g2n, wr3, br, nt=seq // 128, tr=128, gate_blk=2, sc_blk=4,
                                       sh_blk=3, proj_row0=0, intos=[None] * 3, total_rows=n,
                                       name="norm2_prompt", **p_mod)
    x1, h2, route = _resid_norm_router(xs2, proj, mod, g2n, wr3, br, gate_blk=2, sc_blk=4, sh_blk=3,
                                       proj_row0=n_p, intos=[x1, h2, route], total_rows=n,
                                       name="norm2_sample", **s_mod)

    e1 = route[:, 2].astype(jnp.int32)
    e2 = route[:, 3].astype(jnp.int32)
    n_tiles = -(-(2 * n + N_EXPERTS * (MOE_TILE - 1)) // MOE_TILE)
    pos, src, steps = _moe_plan(e1, e2, n_tiles)
    y_sorted = _moe_experts(h2, src, w_gate_e[0], w_up_e[0], w_down_e[0], steps, n_tiles)

    fg = row(final_g)
    y_prompt = _final(x1, y_sorted, pos, route, mod, fg, nt=seq // 128, tr=128, gate_blk=5, row0=0, out_rows=n_p,
                      name="final_prompt", **p_mod)
    y_sample = _final(x1, y_sorted, pos, route, mod, fg, gate_blk=5, row0=n_p, out_rows=ns, name="final_sample",
                      **s_mod)

    return (y_prompt.reshape(nb, seq, d), y_sample.reshape(ns, 1, d),
            hg_p[None], conv_p[None], ssm_p.reshape(1, nb, n_ssm_heads, SSM_HEADDIM, SSM_STATE),
            hg_s[None], conv_s[None], ssm_s[None])
```

```python
import functools

import numpy as np
import jax
import jax.numpy as jnp
from jax import lax
from jax.experimental import pallas as pl
from jax.experimental.pallas import tpu as pltpu

F32 = jnp.float32
BF16 = jnp.bfloat16
EPS = 1e-6

VMEM_LIMIT_BYTES = 56 * 1024 * 1024
LANES = 128

HG_DK = 128
HG_CHUNK = 128
HG_HEADS_PER_STEP = 8
SSM_HEADDIM = 64
SSM_STATE = 128
SSM_HPG = 16
SSM_CHUNK = 128
SSM_CONV = 4
N_GROUPS = 4
EXPERTS_PER_GROUP = 4
N_EXPERTS = 16
MOE_TILE = 128
MOE_UP_VMEM_BYTES = 60 * 1024 * 1024
SLAB_PITCH = 40


def _params(sem, vmem=VMEM_LIMIT_BYTES, flags=None):
    return pltpu.CompilerParams(dimension_semantics=sem, vmem_limit_bytes=vmem, flags=flags)


def _sigmoid(x):
    return 1.0 / (1.0 + jnp.exp(-x))


def _silu(x):
    return x * _sigmoid(x)


def _softplus(x):
    return jnp.maximum(x, 0.0) + jnp.log(1.0 + jnp.exp(-jnp.abs(x)))


def _split3(x):
    p1 = x.astype(BF16)
    r1 = x - p1.astype(F32)
    p2 = r1.astype(BF16)
    r2 = r1 - p2.astype(F32)
    return p1, p2, r2.astype(BF16)


def _dot(a, b):
    return jnp.dot(a, b, preferred_element_type=F32)


def _dot_nt(a, b):
    return lax.dot_general(a, b, (((1,), (1,)), ((), ())), preferred_element_type=F32)


def _dot_tn(a, b):
    return lax.dot_general(a, b, (((0,), (0,)), ((), ())), preferred_element_type=F32)


def _exact_dot(w3_bf16, x_f32):
    return _dot(w3_bf16, jnp.concatenate(_split3(x_f32), axis=0))


def _exact_dot_rhs(x_f32, w3_bf16):
    return _dot(jnp.concatenate(_split3(x_f32), axis=-1), w3_bf16)


def _thrice(w, axis):
    return jnp.concatenate([w, w, w], axis=axis)


def _pcall(body, *, grid, in_specs, args, outs, sem, name, scratch=(), num_prefetch=0, flags=None):
    n_in = len(args) - num_prefetch
    into = [(i, o[3]) for i, o in enumerate(outs) if o[3] is not None]
    in_specs = list(in_specs) + [pl.BlockSpec(memory_space=pl.ANY)] * len(into)
    all_args = list(args) + [buf for _, buf in into]
    aliases = {len(args) + j: i for j, (i, _) in enumerate(into)}

    def wrapped(*refs):
        keep = refs[:num_prefetch + n_in] + refs[num_prefetch + n_in + len(into):]
        return body(*keep)

    grid_spec = pltpu.PrefetchScalarGridSpec(
        num_scalar_prefetch=num_prefetch, grid=grid, in_specs=in_specs,
        out_specs=[o[2] for o in outs], scratch_shapes=list(scratch))
    res = pl.pallas_call(
        wrapped, grid_spec=grid_spec,
        out_shape=[jax.ShapeDtypeStruct(o[0], o[1]) for o in outs],
        input_output_aliases=aliases, compiler_params=_params(sem, flags=flags), name=name,
    )(*all_args)
    return res


def _mm_body(a_ref, b_ref, *rest, n_extra, a_fn, epilogue):
    extras = rest[:n_extra]
    o_ref = rest[n_extra]
    a = a_ref[...]
    if a_fn is not None:
        a = a_fn(a)
    acc = _dot(a.astype(BF16), b_ref[...].astype(BF16))
    if epilogue is not None:
        acc = epilogue(acc, *[e[...] for e in extras])
    o_ref[...] = acc.astype(o_ref.dtype)


def _mm(a, b, *, tm, tn, name, k=None, a_kblk=0, b_kblk=0, col0=0, ncols=None, extras=(), a_fn=None,
        epilogue=None, out_dtype=F32):
    m = a.shape[0]
    k = a.shape[1] if k is None else k
    ncols = b.shape[1] if ncols is None else ncols
    assert m % tm == 0 and ncols % tn == 0 and col0 % tn == 0
    cb = col0 // tn
    in_specs = [pl.BlockSpec((tm, k), lambda i, j: (i, a_kblk), pipeline_mode=pl.Buffered(1)),
                pl.BlockSpec((k, tn), lambda i, j: (b_kblk, j + cb))]
    in_specs += [s for _, s in extras]
    return pl.pallas_call(
        functools.partial(_mm_body, n_extra=len(extras), a_fn=a_fn, epilogue=epilogue),
        grid=(m // tm, ncols // tn),
        in_specs=in_specs,
        out_specs=pl.BlockSpec((tm, tn), lambda i, j: (i, j)),
        out_shape=jax.ShapeDtypeStruct((m, ncols), out_dtype),
        compiler_params=_params(("parallel", "arbitrary")),
        name=name,
    )(a, b, *[x for x, _ in extras])


def _rms(x, g):
    return x * lax.rsqrt(jnp.mean(x * x, axis=-1, keepdims=True) + EPS) * g


def _mod_view(mod, per_row):
    return mod if per_row else mod.reshape(mod.shape[0], 1, mod.shape[1])


def _mod_spec(per_row, tr, d, col_blk, nt, row0):
    if per_row:
        rb = row0 // tr
        return pl.BlockSpec((tr, d), lambda b, t: (rb + t, col_blk))
    return pl.BlockSpec((1, 1, d), lambda b, t: (row0 + b, 0, col_blk))


def _rows_spec(tr, d, nt, row0):
    rb = row0 // tr
    return pl.BlockSpec((tr, d), lambda b, t: (rb + b * nt + t, 0))


def _norm_mod_body(x_ref, sc_ref, sh_ref, g_ref, o_ref):
    d = x_ref.shape[-1]
    sc = sc_ref[...].reshape(-1, d)
    sh = sh_ref[...].reshape(-1, d)
    o_ref[...] = (_rms(x_ref[...], g_ref[...]) * (1.0 + sc) + sh).astype(o_ref.dtype)


def _norm_mod(x, mod, g, *, nb, nt, tr, per_row, mod_row0, sc_blk, sh_blk, into, total_rows, out_row0, name):
    d = x.shape[-1]
    mv = _mod_view(mod, per_row)
    return _pcall(
        _norm_mod_body, grid=(nb, nt),
        in_specs=[_rows_spec(tr, d, nt, 0),
                  _mod_spec(per_row, tr, d, sc_blk, nt, mod_row0),
                  _mod_spec(per_row, tr, d, sh_blk, nt, mod_row0),
                  pl.BlockSpec((1, d), lambda b, t: (0, 0))],
        args=[x, mv, mv, g],
        outs=[((total_rows, d), BF16, _rows_spec(tr, d, nt, out_row0), into)],
        sem=("parallel", "arbitrary"), name=name)[0]


def _router_math(lg):
    lane = lax.broadcasted_iota(jnp.int32, lg.shape, 1)
    neg = jnp.float32(-jnp.inf)
    gm = jnp.where(lane < N_GROUPS, lg, neg)
    gmax = jnp.max(gm, axis=-1, keepdims=True)
    gidx = jnp.min(jnp.where(gm == gmax, lane, LANES), axis=-1, keepdims=True)
    gsum = jnp.sum(jnp.where(lane < N_GROUPS, jnp.exp(gm - gmax), 0.0), axis=-1, keepdims=True)
    p_group = 1.0 / gsum
    lo = N_GROUPS + EXPERTS_PER_GROUP * gidx
    em = jnp.where((lane >= lo) & (lane < lo + EXPERTS_PER_GROUP), lg, neg)
    m1 = jnp.max(em, axis=-1, keepdims=True)
    i1 = jnp.min(jnp.where(em == m1, lane, LANES), axis=-1, keepdims=True)
    em2 = jnp.where(lane == i1, neg, em)
    m2 = jnp.max(em2, axis=-1, keepdims=True)
    i2 = jnp.min(jnp.where(em2 == m2, lane, LANES), axis=-1, keepdims=True)
    r = jnp.exp(m2 - m1)
    w1 = p_group / (1.0 + r)
    w2 = p_group * r / (1.0 + r)
    e1 = (i1 - N_GROUPS).astype(F32)
    e2 = (i2 - N_GROUPS).astype(F32)
    out = jnp.where(lane == 0, w1, 0.0)
    out = jnp.where(lane == 1, w2, out)
    out = jnp.where(lane == 2, e1, out)
    out = jnp.where(lane == 3, e2, out)
    return out


def _resid_norm_router_body(x_ref, p_ref, gt_ref, sc_ref, sh_ref, g_ref, wr_ref, br_ref, x1_ref, h_ref, rt_ref):
    d = x_ref.shape[-1]
    gt = gt_ref[...].reshape(-1, d)
    sc = sc_ref[...].reshape(-1, d)
    sh = sh_ref[...].reshape(-1, d)
    x1 = x_ref[...] + gt * p_ref[...]
    x1_ref[...] = x1
    h = _rms(x1, g_ref[...]) * (1.0 + sc) + sh
    tr = x_ref.shape[0]
    for k in range(SLAB_PITCH):
        slab = h[:, k * LANES:(k + 1) * LANES] if k < d // LANES else jnp.zeros((tr, LANES), F32)
        h_ref[pl.ds(k, tr, stride=SLAB_PITCH), :] = slab
    h1, h2, _ = _split3(h)
    lg = _dot(jnp.concatenate([h1, h1, h2], axis=-1), wr_ref[...])
    rt_ref[...] = _router_math(lg + br_ref[...])


def _resid_norm_router(x, proj, mod, g, wr3, br, *, nb, nt, tr, per_row, mod_row0, gate_blk, sc_blk, sh_blk,
                       proj_row0, intos, total_rows, name):
    d = x.shape[-1]
    mv = _mod_view(mod, per_row)
    ms = lambda blk: _mod_spec(per_row, tr, d, blk, nt, mod_row0)
    orow = _rows_spec(tr, d, nt, proj_row0)
    return _pcall(
        _resid_norm_router_body, grid=(nb, nt),
        in_specs=[_rows_spec(tr, d, nt, 0), orow, ms(gate_blk), ms(sc_blk), ms(sh_blk),
                  pl.BlockSpec((1, d), lambda b, t: (0, 0)),
                  pl.BlockSpec((3 * d, LANES), lambda b, t: (0, 0)),
                  pl.BlockSpec((1, LANES), lambda b, t: (0, 0))],
        args=[x, proj, mv, mv, mv, g, wr3, br],
        outs=[((total_rows, d), F32, orow, intos[0]),
              ((total_rows * SLAB_PITCH, LANES), F32, _rows_spec(tr * SLAB_PITCH, LANES, nt, proj_row0 * SLAB_PITCH),
               intos[1]),
              ((total_rows, LANES), F32, _rows_spec(tr, LANES, nt, proj_row0), intos[2])],
        sem=("parallel", "arbitrary"), name=name)


def _row_gather_start(src_hbm, idx_ref, idx0, dst, sem, n, unrolled=False, rows=1, pitch=1):
    def body(r, carry):
        s0 = idx_ref[idx0 + r] * pitch
        d0 = r * pitch
        if pitch % 8 == 0:
            s0 = pl.multiple_of(s0, 8)
            d0 = d0 if isinstance(d0, int) else pl.multiple_of(d0, 8)
        pltpu.make_async_copy(src_hbm.at[pl.ds(s0, rows), :], dst.at[pl.ds(d0, rows), :], sem).start()
        return carry

    if unrolled:
        for r in range(n):
            body(r, 0)
    else:
        lax.fori_loop(0, n, body, 0, unroll=8)


def _row_gather_wait(src_hbm, dst, sem, n, rows=1, pitch=1):
    def body(r, carry):
        d0 = pl.multiple_of(r * pitch, 8) if pitch % 8 == 0 else r * pitch
        pltpu.make_async_copy(src_hbm.at[pl.ds(0, rows), :], dst.at[pl.ds(d0, rows), :], sem).wait()
        return carry

    lax.fori_loop(0, n, body, 0, unroll=8)


def _final_body(pos_ref, x1_ref, rt_ref, gt_ref, g_ref, y_hbm, o_ref, ybuf, sem, *, tr, tok0, n_tok):
    d = x1_ref.shape[-1]
    i = pl.program_id(0)
    slot = lax.rem(i, 2)

    def start(step, sl):
        for k in range(2):
            _row_gather_start(y_hbm, pos_ref, k * n_tok + tok0 + step * tr, ybuf.at[sl, k], sem.at[sl], tr)

    @pl.when(i == 0)
    def _():
        start(0, 0)

    @pl.when(i + 1 < pl.num_programs(0))
    def _():
        start(i + 1, 1 - slot)

    for k in range(2):
        _row_gather_wait(y_hbm, ybuf.at[slot, k], sem.at[slot], tr)
    gt = gt_ref[...].reshape(-1, d)
    rt = rt_ref[...]
    moe = rt[:, 0:1] * ybuf[slot, 0] + rt[:, 1:2] * ybuf[slot, 1]
    o_ref[...] = _rms(x1_ref[...] + gt * moe, g_ref[...])


def _final(x1, y_sorted, pos, route, mod, g, *, nb, nt, tr, per_row, mod_row0, gate_blk, row0, out_rows, name):
    d = x1.shape[-1]
    n_tok = x1.shape[0]
    rb = row0 // tr
    if per_row:
        mspec = pl.BlockSpec((tr, d), lambda i, pos_: (mod_row0 // tr + i, gate_blk))
    else:
        mspec = pl.BlockSpec((1, 1, d), lambda i, pos_: (mod_row0 + i // nt, 0, gate_blk))
    return _pcall(
        functools.partial(_final_body, tr=tr, tok0=row0, n_tok=n_tok), grid=(nb * nt,),
        in_specs=[pl.BlockSpec((tr, d), lambda i, pos_: (rb + i, 0)),
                  pl.BlockSpec((tr, LANES), lambda i, pos_: (rb + i, 0)),
                  mspec,
                  pl.BlockSpec((1, d), lambda i, pos_: (0, 0)),
                  pl.BlockSpec(memory_space=pl.ANY)],
        args=[pos, x1, route, _mod_view(mod, per_row), g, y_sorted],
        outs=[((out_rows, d), F32, pl.BlockSpec((tr, d), lambda i, pos_: (i, 0)), None)],
        sem=("arbitrary",), name=name, num_prefetch=1,
        scratch=[pltpu.VMEM((2, 2, tr, d), F32), pltpu.SemaphoreType.DMA((2,))])[0]


HG_MXU_LEVELS = 3


def _hgrn_tables(c):
    nlev = int(np.log2(c))
    t = np.arange(c)[:, None]
    u = np.arange(c)[None, :]
    mats = [u <= t]
    masks = []
    for l in range(nlev):
        m = 1 << l
        p = (t // (2 * m)) * (2 * m) + m - 1
        upper = ((t >> l) & 1) == 1
        if l < HG_MXU_LEVELS:
            mats.append(np.where(upper, (u > p) & (u <= t), (u > t) & (u <= p)))
        same = (t >> (l + 1)) == (u >> (l + 1))
        masks.append(same & upper & (((u >> l) & 1) == 0))
    masks.append(t == u)
    w = np.concatenate([m.astype(np.float32) for m in mats], axis=0)
    return jnp.asarray(np.tile(w, (1, 3)), BF16), jnp.asarray(np.stack(masks).astype(np.float32))


def _hgrn_prompt_body(q_ref, f_ref, v_ref, og_ref, lb_ref, gn_ref, w_ref, mk_ref, o_ref, s_ref, st_scr, *,
                      nchunks, nlev):
    c = HG_CHUNK
    hp = st_scr.shape[0]
    t = pl.program_id(2)

    @pl.when(t == 0)
    def _():
        st_scr[...] = jnp.zeros_like(st_scr)

    lb = lb_ref[...]
    gn = gn_ref[...]
    trow = lax.broadcasted_iota(jnp.int32, (c, HG_DK), 0)
    upper = [((trow >> l) & 1) == 1 for l in range(HG_MXU_LEVELS, nlev)]

    def chunk(ci, carry):
        r0 = pl.multiple_of(ci * c, c)
        rows = pl.ds(r0, c)
        lns = [slice(i * HG_DK, (i + 1) * HG_DK) for i in range(hp)]
        f = [lb[:, ln] + (1.0 - lb[:, ln]) * _sigmoid(f_ref[rows, ln]) for ln in lns]
        g = [_exact_dot(w_ref[...], jnp.log2(fi)) for fi in f]
        kk = [1.0 - fi for fi in f]
        q = [_silu(q_ref[rows, ln]) for ln in lns]
        v = [v_ref[rows, ln].astype(BF16) for ln in lns]
        b = [gi[0:c] for gi in g]
        st = [st_scr[i] for i in range(hp)]
        qb = [qi.astype(BF16) for qi in q]
        kb = [ki.astype(BF16) for ki in kk]
        o = [_dot_nt(qb[i] * jnp.exp2(b[i]).astype(BF16), st[i].astype(BF16)) for i in range(hp)]
        scores = [mk_ref[nlev] * _dot_nt(qb[i], kb[i]) for i in range(hp)]
        for l in range(nlev):
            for i in range(hp):
                if l < HG_MXU_LEVELS:
                    x = jnp.exp2(g[i][(1 + l) * c:(2 + l) * c])
                else:
                    m = 1 << l
                    b3 = b[i].reshape(c // (2 * m), 2 * m, HG_DK)
                    d = (b3 - b3[:, m - 1:m, :]).reshape(c, HG_DK)
                    x = jnp.exp2(jnp.where(upper[l - HG_MXU_LEVELS], d, -d))
                x = x.astype(BF16)
                scores[i] = scores[i] + mk_ref[l] * _dot_nt(qb[i] * x, kb[i] * x)
        for i, ln in enumerate(lns):
            b_end = b[i][c - 1:c, :]
            oi = o[i] + _dot(scores[i].astype(BF16), v[i])
            kw = kb[i] * jnp.exp2(b_end - b[i]).astype(BF16)
            st_scr[i] = st[i] * jnp.exp2(b_end) + _dot_tn(v[i], kw)
            oi = _rms(oi, gn[:, ln]) * _silu(og_ref[rows, ln])
            o_ref[rows, ln] = oi.astype(o_ref.dtype)
        return carry

    lax.fori_loop(0, nchunks, chunk, 0)

    @pl.when(t == pl.num_programs(2) - 1)
    def _():
        for i in range(hp):
            s_ref[0, i] = st_scr[i].T


def _hgrn_prompt(u, lb, gn, *, nb, seq, nheads, col_q, col_f, col_v, col_og, total_rows, tb):
    c = HG_CHUNK
    nt = seq // tb
    nlev = int(np.log2(c))
    w, mk = _hgrn_tables(c)
    hp = HG_HEADS_PER_STEP
    wd = hp * HG_DK
    cs = lambda col: pl.BlockSpec((tb, wd), lambda b, h, t: (b * nt + t, col // wd + h))
    hv = pl.BlockSpec((1, wd), lambda b, h, t: (0, h))
    return _pcall(
        functools.partial(_hgrn_prompt_body, nchunks=tb // c, nlev=nlev),
        grid=(nb, nheads // hp, nt),
        in_specs=[cs(col_q), cs(col_f), cs(col_v), cs(col_og), hv, hv,
                  pl.BlockSpec(w.shape, lambda b, h, t: (0, 0)),
                  pl.BlockSpec(mk.shape, lambda b, h, t: (0, 0, 0))],
        args=[u, u, u, u, lb, gn, w, mk],
        outs=[((total_rows, nheads * HG_DK), BF16, pl.BlockSpec((tb, wd), lambda b, h, t: (b * nt + t, h)), None),
              ((nb, nheads, HG_DK, HG_DK), F32, pl.BlockSpec((1, hp, HG_DK, HG_DK), lambda b, h, t: (b, h, 0, 0)), None)],
        sem=("parallel", "parallel", "arbitrary"), name="hgrn_prompt",
        scratch=[pltpu.VMEM((hp, HG_DK, HG_DK), F32)])


def _ssm_expand_tables(n_heads):
    ng = n_heads // SSM_HPG
    w2, w1 = SSM_HPG * LANES, SSM_HPG * SSM_HEADDIM
    e = np.zeros((ng, n_heads, w2 + w1), np.float32)
    for g in range(ng):
        for i in range(SSM_HPG):
            e[g, g * SSM_HPG + i, i * LANES:(i + 1) * LANES] = 1.0
            e[g, g * SSM_HPG + i, w2 + i * SSM_HEADDIM:w2 + (i + 1) * SSM_HEADDIM] = 1.0
    return jnp.asarray(np.tile(e, (1, 3, 1)), BF16)


def _conv_silu(buf_ref, w, bias, r0, n):
    b = buf_ref[pl.ds(r0, n + 8), :]
    acc = b * w[0:1, :]
    for j in range(1, SSM_CONV):
        acc = pltpu.roll(acc, 1, axis=0) + b * w[j:j + 1, :]
    return _silu(acc[8:, :] + bias)


def _ssd_prompt_body(x_ref, b_ref, c_ref, z_ref, dt_ref, wx_ref, wb_ref, wc_ref, bx_ref, bb_ref, bc_ref,
                     dtb_ref, alog_ref, dsk_ref, e_ref, gn_ref, y_ref, s_ref,
                     ht_scr, cbx, cbb, cbc, *, nchunks):
    L = SSM_CHUNK
    tb = nchunks * L
    w2 = SSM_HPG * LANES
    t = pl.program_id(2)

    @pl.when(t == 0)
    def _():
        ht_scr[...] = jnp.zeros_like(ht_scr)
        cbx[0:8, :] = jnp.zeros((8, cbx.shape[1]), F32)
        cbb[0:8, :] = jnp.zeros((8, cbb.shape[1]), F32)
        cbc[0:8, :] = jnp.zeros((8, cbc.shape[1]), F32)

    cbx[8:8 + tb, :] = x_ref[...]
    cbb[8:8 + tb, :] = b_ref[...]
    cbc[8:8 + tb, :] = c_ref[...]

    e_all = e_ref[0]
    e1 = e_all[:, w2:]
    neg_a = -jnp.exp(alog_ref[...])
    dsk1 = _exact_dot_rhs(dsk_ref[...], e1)
    row = lax.broadcasted_iota(jnp.int32, (L, L), 0)
    col = lax.broadcasted_iota(jnp.int32, (L, L), 1)
    causal = col <= row
    tril = _thrice(jnp.where(causal, 1.0, 0.0).astype(BF16), 1)
    lane = lax.broadcasted_iota(jnp.int32, (L, LANES), 1)
    lo_half = lane < SSM_HEADDIM

    for ci in range(nchunks):
        r0 = ci * L
        xs = _conv_silu(cbx, wx_ref[...], bx_ref[...], r0, L)
        bm = _conv_silu(cbb, wb_ref[...], bb_ref[...], r0, L).astype(BF16)
        cm = _conv_silu(cbc, wc_ref[...], bc_ref[...], r0, L).astype(BF16)
        dt = _softplus(dt_ref[r0:r0 + L, :] + dtb_ref[...])
        cum = _exact_dot(tril, dt * neg_a)
        cum_e = _exact_dot_rhs(cum, e_all)
        cum1 = cum_e[:, w2:]
        dt1 = _exact_dot_rhs(dt, e1)
        xdt = xs * dt1
        cum_end = cum1[L - 1:L, :]
        ht = ht_scr[...]
        y = _dot(cm, ht.astype(BF16)) * jnp.exp(cum1)
        cb = _dot_nt(cm, bm)
        parts = []
        for j in range(SSM_HPG // 2):
            xp = xdt[:, j * LANES:(j + 1) * LANES]
            acc = None
            for half in range(2):
                i = 2 * j + half
                colb = cum_e[:, i * LANES:(i + 1) * LANES]
                seg = colb - colb.T
                m = (cb * jnp.exp(jnp.where(causal, seg, -jnp.inf))).astype(BF16)
                xh = jnp.where(lo_half if half == 0 else ~lo_half, xp, 0.0).astype(BF16)
                d = _dot(m, xh)
                acc = d if acc is None else acc + d
            parts.append(acc)
        y = y + jnp.concatenate(parts, axis=-1) + dsk1 * xs
        y = y * _silu(z_ref[r0:r0 + L, :])
        y_ref[r0:r0 + L, :] = _rms(y, gn_ref[...]).astype(y_ref.dtype)
        xw = (xdt * jnp.exp(cum_end - cum1)).astype(BF16)
        ht_scr[...] = ht * jnp.exp(cum_end) + _dot_tn(bm, xw)

    cbx[0:8, :] = cbx[tb:tb + 8, :]
    cbb[0:8, :] = cbb[tb:tb + 8, :]
    cbc[0:8, :] = cbc[tb:tb + 8, :]

    @pl.when(t == pl.num_programs(2) - 1)
    def _():
        s_ref[0, 0] = ht_scr[...].T


def _ssd_prompt(u, u_dt, conv_w, conv_b, dt_bias, a_log, d_skip, gn, *, nb, seq, col_z, col_x, total_rows, tb):
    n_heads = u_dt.shape[1]
    ng = n_heads // SSM_HPG
    w1 = SSM_HPG * SSM_HEADDIM
    inner = ng * w1
    nt = seq // tb
    e = _ssm_expand_tables(n_heads)
    xb, bb, cb_, zb = col_x // w1, (col_x + inner) // LANES, (col_x + inner + ng * SSM_STATE) // LANES, col_z // w1
    wide = lambda blk0: pl.BlockSpec((tb, w1), lambda b, g, t: (b * nt + t, blk0 + g))
    thin = lambda blk0: pl.BlockSpec((tb, LANES), lambda b, g, t: (b * nt + t, blk0 + g))
    cw = lambda rows, width, blk0: pl.BlockSpec((rows, width), lambda b, g, t: (0, blk0 + g))
    full = lambda a: pl.BlockSpec(a.shape, lambda b, g, t: (0,) * a.ndim)
    return _pcall(
        functools.partial(_ssd_prompt_body, nchunks=tb // SSM_CHUNK),
        grid=(nb, ng, nt),
        in_specs=[wide(xb), thin(bb), thin(cb_), wide(zb),
                  pl.BlockSpec((tb, n_heads), lambda b, g, t: (b * nt + t, 0)),
                  cw(SSM_CONV, w1, 0), cw(SSM_CONV, LANES, inner // LANES),
                  cw(SSM_CONV, LANES, (inner + ng * SSM_STATE) // LANES),
                  cw(1, w1, 0), cw(1, LANES, inner // LANES), cw(1, LANES, (inner + ng * SSM_STATE) // LANES),
                  full(dt_bias), full(a_log), full(d_skip),
                  pl.BlockSpec((1,) + e.shape[1:], lambda b, g, t: (g, 0, 0)),
                  cw(1, w1, 0)],
        args=[u, u, u, u, u_dt, conv_w, conv_w, conv_w, conv_b, conv_b, conv_b, dt_bias, a_log, d_skip, e, gn],
        outs=[((total_rows, inner), BF16, pl.BlockSpec((tb, w1), lambda b, g, t: (b * nt + t, g)), None),
              ((nb, ng, w1, SSM_STATE), F32, pl.BlockSpec((1, 1, w1, SSM_STATE), lambda b, g, t: (b, g, 0, 0)), None)],
        sem=("parallel", "parallel", "arbitrary"), name="ssd_prompt",
        scratch=[pltpu.VMEM((SSM_STATE, w1), F32), pltpu.VMEM((tb + 8, w1), F32),
                 pltpu.VMEM((tb + 8, LANES), F32), pltpu.VMEM((tb + 8, LANES), F32)])


STEP_TOKENS = 16
STEP_HEADS = 4


def _hgrn_step_body(q_ref, v_ref, og_ref, ft_ref, lbc_ref, gn_ref, s_ref, o_ref, so_ref):
    lbc = lbc_ref[...]
    f = lbc + (1.0 - lbc) * _sigmoid(ft_ref[0])
    kk = 1.0 - f
    q = _silu(q_ref[...])
    v = v_ref[...]
    gn = gn_ref[...]
    og = og_ref[...]
    for i in range(STEP_HEADS):
        ln = slice(i * HG_DK, (i + 1) * HG_DK)
        rows = []
        for j in range(STEP_TOKENS):
            sn = f[ln, j:j + 1] * s_ref[j, i] + kk[ln, j:j + 1] * v[j:j + 1, ln]
            so_ref[j, i] = sn
            qj = jnp.broadcast_to(q[j:j + 1, ln], (16, HG_DK)).astype(BF16)
            rows.append(_dot(qj, sn.astype(BF16))[0:1, :])
        o = jnp.concatenate(rows, axis=0)
        o_ref[:, ln] = (_rms(o, gn[:, ln]) * _silu(og[:, ln])).astype(o_ref.dtype)


def _hgrn_step(u, f_t, lb_col, gn, state, *, row0, nheads, col_q, col_v, col_og, into):
    n = state.shape[0]
    tb = STEP_TOKENS
    hp = STEP_HEADS
    wd = hp * HG_DK
    rb = row0 // tb
    cs = lambda col: pl.BlockSpec((tb, wd), lambda b, h: (rb + b, col // wd + h))
    sspec = pl.BlockSpec((tb, hp, HG_DK, HG_DK), lambda b, h: (b, h, 0, 0))
    return _pcall(
        _hgrn_step_body, grid=(n // tb, nheads // hp),
        in_specs=[cs(col_q), cs(col_v), cs(col_og),
                  pl.BlockSpec((1, wd, tb), lambda b, h: (b, h, 0)),
                  pl.BlockSpec((wd, 1), lambda b, h: (h, 0)),
                  pl.BlockSpec((1, wd), lambda b, h: (0, h)),
                  sspec],
        args=[u, u, u, f_t, lb_col, gn, state],
        outs=[(into.shape, into.dtype, pl.BlockSpec((tb, wd), lambda b, h: (rb + b, h)), into),
              (state.shape, F32, sspec, None)],
        sem=("parallel", "arbitrary"), name="hgrn_step")


def _ssd_prep_body(x_ref, b_ref, c_ref, sx0, sx1, sx2, sb0, sb1, sb2, sc0, sc1, sc2, dt_ref,
                   wx_ref, wb_ref, wc_ref, bx_ref, bb_ref, bc_ref, dtb_ref, alog_ref, e_ref,
                   xs_ref, bm_ref, cm_ref, xdt_ref, da_ref):
    def conv(taps, cur, w_ref, bias_ref):
        w = w_ref[...]
        acc = bias_ref[...] + cur[...] * w[SSM_CONV - 1:SSM_CONV, :]
        for j, tap in enumerate(taps):
            acc = acc + tap[...] * w[j:j + 1, :]
        return _silu(acc)

    xs = conv((sx0, sx1, sx2), x_ref, wx_ref, bx_ref)
    xs_ref[...] = xs
    bm_ref[...] = conv((sb0, sb1, sb2), b_ref, wb_ref, bb_ref)
    cm_ref[...] = conv((sc0, sc1, sc2), c_ref, wc_ref, bc_ref)
    e1 = e_ref[0][:, SSM_HPG * LANES:]
    dt = _softplus(dt_ref[...] + dtb_ref[...])
    xdt_ref[...] = xs * _exact_dot_rhs(dt, e1)
    da_ref[...] = _exact_dot_rhs(jnp.exp(dt * -jnp.exp(alog_ref[...])), e1)


def _ssd_prep(u, u_dt, conv_state2d, conv_w, conv_b, dt_bias, a_log, *, row0, n, col_x):
    n_heads = u_dt.shape[1]
    ng = n_heads // SSM_HPG
    w1 = SSM_HPG * SSM_HEADDIM
    inner = ng * w1
    cdim = inner + 2 * ng * SSM_STATE
    e = _ssm_expand_tables(n_heads)
    rb = row0 // n
    xoff, boff, coff = 0, inner, inner + ng * SSM_STATE
    cur = lambda width, off: pl.BlockSpec((n, width), lambda g: (rb, (col_x + off) // width + g))
    tap = lambda width, off, j: pl.BlockSpec((n, width), lambda g: (0, (j * cdim + off) // width + g))
    cw = lambda rows, width, off: pl.BlockSpec((rows, width), lambda g: (0, off // width + g))
    full = lambda a: pl.BlockSpec(a.shape, lambda g: (0,) * a.ndim)
    taps = [tap(w1, xoff, j) for j in range(3)] + [tap(LANES, boff, j) for j in range(3)] + \
           [tap(LANES, coff, j) for j in range(3)]
    o_wide = pl.BlockSpec((n, w1), lambda g: (0, g))
    o_thin = pl.BlockSpec((n, LANES), lambda g: (0, g))
    return _pcall(
        _ssd_prep_body, grid=(ng,),
        in_specs=[cur(w1, xoff), cur(LANES, boff), cur(LANES, coff)] + taps +
                 [pl.BlockSpec((n, n_heads), lambda g: (rb, 0)),
                  cw(SSM_CONV, w1, xoff), cw(SSM_CONV, LANES, boff), cw(SSM_CONV, LANES, coff),
                  cw(1, w1, xoff), cw(1, LANES, boff), cw(1, LANES, coff),
                  full(dt_bias), full(a_log),
                  pl.BlockSpec((1,) + e.shape[1:], lambda g: (g, 0, 0))],
        args=[u, u, u] + [conv_state2d] * 9 + [u_dt, conv_w, conv_w, conv_w, conv_b, conv_b, conv_b,
                                               dt_bias, a_log, e],
        outs=[((n, inner), F32, o_wide, None), ((n, ng * SSM_STATE), F32, o_thin, None),
              ((n, ng * SSM_STATE), F32, o_thin, None), ((n, inner), F32, o_wide, None),
              ((n, inner), F32, o_wide, None)],
        sem=("arbitrary",), name="ssd_prep")


def _ssd_step_body(xs_ref, bm_ref, cm_ref, z_ref, xdt_ref, da_ref, dsk_ref, gn_ref, s_ref, y_ref, so_ref):
    w1 = SSM_HPG * SSM_HEADDIM
    bm = bm_ref[...]
    cm = cm_ref[...]
    xdt = xdt_ref[0]
    da = da_ref[0]
    rows = []
    for j in range(STEP_TOKENS):
        h = s_ref[j].reshape(w1, SSM_STATE)
        hn = da[:, j:j + 1] * h + xdt[:, j:j + 1] * bm[j:j + 1, :]
        so_ref[j] = hn.reshape(SSM_HPG, SSM_HEADDIM, SSM_STATE)
        cj = jnp.broadcast_to(cm[j:j + 1, :], (16, SSM_STATE)).astype(BF16)
        rows.append(_dot_nt(cj, hn.astype(BF16))[0:1, :])
    y = jnp.concatenate(rows, axis=0) + dsk_ref[...] * xs_ref[...]
    y = y * _silu(z_ref[...])
    y_ref[...] = _rms(y, gn_ref[...]).astype(y_ref.dtype)


def _ssd_step(u, xs, bm, cm, xdt_t, da_t, dsk_full, gn, state, *, row0, col_z, into):
    n, n_heads = state.shape[0], state.shape[1]
    ng = n_heads // SSM_HPG
    w1 = SSM_HPG * SSM_HEADDIM
    tb = STEP_TOKENS
    rb = row0 // tb
    sspec = pl.BlockSpec((tb, SSM_HPG, SSM_HEADDIM, SSM_STATE), lambda b, g: (b, g, 0, 0))
    colspec = pl.BlockSpec((1, w1, tb), lambda b, g: (b, g, 0))
    return _pcall(
        _ssd_step_body, grid=(n // tb, ng),
        in_specs=[pl.BlockSpec((tb, w1), lambda b, g: (b, g)),
                  pl.BlockSpec((tb, SSM_STATE), lambda b, g: (b, g)),
                  pl.BlockSpec((tb, SSM_STATE), lambda b, g: (b, g)),
                  pl.BlockSpec((tb, w1), lambda b, g: (rb + b, col_z // w1 + g)),
                  colspec, colspec,
                  pl.BlockSpec((1, w1), lambda b, g: (0, g)),
                  pl.BlockSpec((1, w1), lambda b, g: (0, g)),
                  sspec],
        args=[xs, bm, cm, u, xdt_t, da_t, dsk_full, gn, state],
        outs=[(into.shape, into.dtype, pl.BlockSpec((tb, w1), lambda b, g: (rb + b, g)), into),
              (state.shape, F32, sspec, None)],
        sem=("parallel", "arbitrary"), name="ssd_step")


def _moe_plan(e1, e2, n_tiles):
    n = e1.shape[0]
    tile = MOE_TILE
    eid = jnp.concatenate([e1, e2]).astype(jnp.int32)
    onehot = (eid[:, None] == jnp.arange(N_EXPERTS, dtype=jnp.int32)[None, :]).astype(jnp.int32)
    blk = 128
    assert (2 * n) % blk == 0
    oh3 = onehot.astype(F32).reshape(2 * n // blk, blk, N_EXPERTS)
    tri = jnp.tril(jnp.ones((blk, blk), F32), -1)
    inblk = jnp.einsum("ts,bse->bte", tri, oh3, preferred_element_type=F32)
    btot = jnp.sum(oh3, axis=1)
    before = (jnp.cumsum(btot, axis=0) - btot)[:, None, :] + inblk
    rank = jnp.sum(before.reshape(2 * n, N_EXPERTS) * onehot, axis=1).astype(jnp.int32)
    counts = jnp.sum(onehot, axis=0)
    ptiles = (counts + tile - 1) // tile
    tstart = jnp.cumsum(ptiles) - ptiles
    pos = (tstart * tile)[eid] + rank
    tok = jnp.concatenate([jnp.arange(n, dtype=jnp.int32)] * 2)
    src = jnp.zeros((n_tiles * tile,), jnp.int32).at[pos].set(tok)
    n_used = jnp.sum(ptiles)
    s = jnp.arange(n_tiles, dtype=jnp.int32)
    tend = jnp.cumsum(ptiles)
    ex = jnp.sum((tend[None, :] <= jnp.minimum(s, n_used - 1)[:, None]).astype(jnp.int32), axis=1)
    active = (s < n_used).astype(jnp.int32)
    first = jnp.concatenate([jnp.ones((1,), jnp.int32), (ex[1:] != ex[:-1]).astype(jnp.int32)])
    nstep = tend[ex]
    has_next = (nstep < n_used).astype(jnp.int32)
    nex = jnp.where(has_next == 1, ex[jnp.minimum(nstep, n_tiles - 1)], ex)
    i32 = lambda a: a.astype(jnp.int32)
    return i32(pos), src, (i32(ex), i32(nex), has_next, first, active)


def _moe_up_body(ex_ref, nex_ref, hasn_ref, first_ref, act_ref, src_ref, x_hbm, wg_hbm, wu_hbm, o_ref,
                 stage_g, stage_u, wg_s, wu_s, xbuf, sem, wsem):
    s = pl.program_id(0)
    last = pl.num_programs(0) - 1
    slot = lax.rem(s, 2)
    tile = o_ref.shape[0]
    n_slab = wg_s.shape[0] // LANES
    slab = dict(rows=n_slab, pitch=SLAB_PITCH)
    nxt = jnp.minimum(s + 1, last)

    def start(step, sl, unrolled=False):
        _row_gather_start(x_hbm, src_ref, step * tile, xbuf.at[sl], sem.at[sl], tile, unrolled, **slab)

    def weight_copies(e):
        return (pltpu.make_async_copy(wg_hbm.at[e], stage_g, wsem.at[0]),
                pltpu.make_async_copy(wu_hbm.at[e], stage_u, wsem.at[1]))

    @pl.when(s == 0)
    def _():
        start(0, 0)
        for cp in weight_copies(ex_ref[0]):
            cp.start(priority=1)

    @pl.when(first_ref[s] == 1)
    def _():
        for cp, stage, dst in zip(weight_copies(ex_ref[s]), (stage_g, stage_u), (wg_s, wu_s)):
            cp.wait()
            dst[...] = stage[...].astype(BF16)

        @pl.when(hasn_ref[s] == 1)
        def _():
            for cp in weight_copies(nex_ref[s]):
                cp.start(priority=1)

    _row_gather_wait(x_hbm, xbuf.at[slot], sem.at[slot], tile, **slab)

    @pl.when(act_ref[s] == 1)
    def _():
        xb = xbuf.at[slot]
        x = jnp.concatenate([xb[pl.ds(k, tile, stride=SLAB_PITCH), :] for k in range(n_slab)], axis=-1)
        x = x.astype(BF16)
        start(nxt, 1 - slot, unrolled=True)
        o_ref[...] = (_silu(_dot(x, wg_s[...])) * _dot(x, wu_s[...])).astype(o_ref.dtype)

    @pl.when(act_ref[s] == 0)
    def _():
        start(nxt, 1 - slot)
        o_ref[...] = jnp.zeros_like(o_ref)

    @pl.when(s == last)
    def _():
        _row_gather_wait(x_hbm, xbuf.at[1 - slot], sem.at[1 - slot], tile, **slab)


def _moe_down_body(ex_ref, nex_ref, hasn_ref, first_ref, act_ref, a_ref, wd_ref, o_ref, wd_s):
    s = pl.program_id(0)

    @pl.when(first_ref[s] == 1)
    def _():
        wd_s[...] = wd_ref[0].astype(BF16)

    @pl.when(act_ref[s] == 1)
    def _():
        o_ref[...] = _dot(a_ref[...], wd_s[...])

    @pl.when(act_ref[s] == 0)
    def _():
        o_ref[...] = jnp.zeros_like(o_ref)


def _moe_experts(x, src, w_gate, w_up, w_down, steps, n_tiles):
    tile = MOE_TILE
    d, de = w_gate.shape[1], w_gate.shape[2]
    rows = n_tiles * tile
    any_spec = pl.BlockSpec(memory_space=pl.ANY)
    act = pl.pallas_call(
        _moe_up_body,
        grid_spec=pltpu.PrefetchScalarGridSpec(
            num_scalar_prefetch=6, grid=(n_tiles,),
            in_specs=[any_spec, any_spec, any_spec],
            out_specs=pl.BlockSpec((tile, de), lambda s, *_: (s, 0)),
            scratch_shapes=[pltpu.VMEM((d, de), F32), pltpu.VMEM((d, de), F32),
                            pltpu.VMEM((d, de), BF16), pltpu.VMEM((d, de), BF16),
                            pltpu.VMEM((2, tile * SLAB_PITCH, LANES), F32),
                            pltpu.SemaphoreType.DMA((2,)), pltpu.SemaphoreType.DMA((2,))]),
        out_shape=jax.ShapeDtypeStruct((rows, de), BF16),
        compiler_params=_params(("arbitrary",), vmem=MOE_UP_VMEM_BYTES), name="moe_up",
    )(*steps, src, x, w_gate, w_up)
    wmap = lambda s, ex, nex, hasn, fi, ac: (jnp.where(fi[s] == 1, ex[s], nex[s]), 0, 0)
    y = pl.pallas_call(
        _moe_down_body,
        grid_spec=pltpu.PrefetchScalarGridSpec(
            num_scalar_prefetch=5, grid=(n_tiles,),
            in_specs=[pl.BlockSpec((tile, de), lambda s, *_: (s, 0)), pl.BlockSpec((1, de, d), wmap)],
            out_specs=pl.BlockSpec((tile, d), lambda s, *_: (s, 0)),
            scratch_shapes=[pltpu.VMEM((de, d), BF16)]),
        out_shape=jax.ShapeDtypeStruct((rows, d), F32),
        compiler_params=_params(("arbitrary",)), name="moe_down",
    )(*steps, act, w_down)
    return y


MM_TM = 1664
MM_TN = 256
MM_TN_WIDE = 512


def kernel(x_prompt, x_sample, state_hgrn, state_conv, state_ssm, c_prompt, c_sample, lb_params, w_mod, b_mod,
           norm1_g, norm2_g, w_in, hg_norm_g, conv_w, conv_b, dt_bias, a_log, d_skip, ssm_norm_g, w_branch_a,
           w_branch_b, w_out, w_router_group, b_router_group, w_router_expert, b_router_expert, w_gate_e, w_up_e,
           w_down_e, final_g):
    nb, seq, d = x_prompt.shape
    ns = x_sample.shape[0]
    assert w_mod.shape[0] == 1 and x_sample.shape[1] == 1
    n_p = nb * seq
    n = n_p + ns
    nheads = state_hgrn.shape[2]
    hgw = nheads * HG_DK
    n_ssm_heads = state_ssm.shape[2]
    inner = n_ssm_heads * SSM_HEADDIM
    ng = n_ssm_heads // SSM_HPG
    cdim = inner + 2 * ng * SSM_STATE
    col_q, col_f, col_v, col_og = 0, hgw, 2 * hgw, 2 * hgw + d
    col_z = 2 * hgw + 2 * d
    col_x = col_z + inner
    n_main = col_x + cdim
    col_gab = n_main + n_ssm_heads
    row = lambda a: a.reshape(1, -1)

    lb = jnp.cumsum(jax.nn.softmax(lb_params.astype(F32), axis=0), axis=0)[0]

    n_c = ns + nb
    n_c_pad = -(-n_c // 8) * 8
    c_all = jnp.concatenate([c_sample, c_prompt, jnp.zeros((n_c_pad - n_c, d), F32)], axis=0)
    mod = _mm(c_all, w_mod[0], tm=n_c_pad, tn=512, name="mod", a_fn=_silu,
              extras=[(row(b_mod[0]), pl.BlockSpec((1, 512), lambda i, j: (0, j)))],
              epilogue=lambda acc, b: acc + b)
    p_mod = dict(nb=nb, per_row=False, mod_row0=ns)
    s_mod = dict(nb=1, nt=1, tr=ns, per_row=True, mod_row0=0)

    xp2 = x_prompt.reshape(n_p, d)
    xs2 = x_sample.reshape(ns, d)
    h = _norm_mod(xp2, mod, row(norm1_g[0]), nt=seq // 256, tr=256, sc_blk=1, sh_blk=0, into=None, total_rows=n,
                  out_row0=0, name="norm1_prompt", **p_mod)
    h = _norm_mod(xs2, mod, row(norm1_g[0]), sc_blk=1, sh_blk=0, into=h, total_rows=n, out_row0=n_p,
                  name="norm1_sample", **s_mod)

    assert n_ssm_heads == LANES
    u = _mm(h, w_in[0], tm=MM_TM, tn=MM_TN_WIDE, ncols=n_main, name="in_proj")
    u_t = _mm(h, w_in[0], tm=MM_TM, tn=MM_TN_WIDE, col0=n_main, ncols=2 * d, name="in_proj_tail")
    u_l = _mm(h, w_in[0], tm=MM_TM, tn=LANES, col0=n_main + 2 * d, ncols=LANES, name="in_proj_last")
    u_dt = u_t[:, :LANES]

    gn_h = row(hg_norm_g[0])
    o_a, hg_p = _hgrn_prompt(u, row(lb), gn_h, nb=nb, seq=seq, nheads=nheads, col_q=col_q, col_f=col_f,
                             col_v=col_v, col_og=col_og, total_rows=n, tb=512)
    f_t = u[n_p:, col_f:col_f + hgw].reshape(ns // STEP_TOKENS, STEP_TOKENS, hgw).transpose(0, 2, 1)
    o_a, hg_s = _hgrn_step(u, f_t, lb.reshape(hgw, 1), gn_h, state_hgrn[0], row0=n_p, nheads=nheads,
                           col_q=col_q, col_v=col_v, col_og=col_og, into=o_a)

    gn_s = row(ssm_norm_g[0])
    y_b, ssm_p = _ssd_prompt(u, u_dt, conv_w[0], row(conv_b[0]), row(dt_bias[0]), row(a_log[0]), row(d_skip[0]),
                             gn_s, nb=nb, seq=seq, col_z=col_z, col_x=col_x, total_rows=n, tb=256)
    xs, bm, cm, xdt, da = _ssd_prep(u, u_dt, state_conv[0].reshape(ns, (SSM_CONV - 1) * cdim), conv_w[0],
                                    row(conv_b[0]), row(dt_bias[0]), row(a_log[0]), row0=n_p, n=ns, col_x=col_x)
    cols = lambda a: a.reshape(ns // STEP_TOKENS, STEP_TOKENS, inner).transpose(0, 2, 1)
    dsk_full = jnp.repeat(d_skip[0], SSM_HEADDIM).reshape(1, inner)
    y_b, ssm_s = _ssd_step(u, xs, bm, cm, cols(xdt), cols(da), dsk_full, gn_s, state_ssm[0], row0=n_p,
                           col_z=col_z, into=y_b)
    xbc_rows = lambda r0, r1: lax.slice(u, (r0, col_x), (r1, col_x + cdim))
    conv_p = jnp.stack([xbc_rows((b + 1) * seq - (SSM_CONV - 1), (b + 1) * seq) for b in range(nb)])
    conv_s = jnp.concatenate([state_conv[0][:, 1:], xbc_rows(n_p, n)[:, None, :]], axis=1)

    tile_spec = lambda off: pl.BlockSpec((MM_TM, MM_TN), lambda i, j: (i, j + off // MM_TN))
    last_blk = u_t.shape[1] // LANES - 1
    half_spec = lambda blk0: pl.BlockSpec((MM_TM, LANES), lambda i, j: (i, jnp.minimum(blk0 + 2 * j, last_blk)))
    n_ct = d // MM_TN

    def gate_a(acc, g0, g1):
        return _sigmoid(jnp.concatenate([g0, g1], axis=-1)) * acc

    def gate_b(acc, g0, g1, gl, ta, pb):
        g1 = jnp.where(pl.program_id(1) == n_ct - 1, gl, g1)
        return ta + _sigmoid(jnp.concatenate([g0, g1], axis=-1)) * (pb + acc)

    t_a = _mm(o_a, w_branch_a[0], tm=MM_TM, tn=MM_TN, name="branch_a",
              extras=[(u_t, half_spec(1)), (u_t, half_spec(2))], epilogue=gate_a)
    half = inner // 2
    p_b = _mm(y_b, w_branch_b[0], tm=MM_TM, tn=MM_TN_WIDE, k=half, name="branch_b_lo")
    gb0 = 1 + d // LANES
    merged = _mm(y_b, w_branch_b[0], tm=MM_TM, tn=MM_TN, k=half, a_kblk=1, b_kblk=1, name="branch_b_hi",
                 extras=[(u_t, half_spec(gb0)), (u_t, half_spec(gb0 + 1)),
                         (u_l, pl.BlockSpec((MM_TM, LANES), lambda i, j: (i, 0))),
                         (t_a, tile_spec(0)), (p_b, tile_spec(0))],
                 epilogue=gate_b, out_dtype=BF16)
    proj = _mm(merged, w_out[0], tm=MM_TM, tn=MM_TN_WIDE, name="out_proj")

    wr = jnp.concatenate([w_router_group[0], w_router_expert[0],
                          jnp.zeros((d, LANES - N_GROUPS - N_EXPERTS), F32)], axis=1)
    wr_hi, wr_lo, _ = _split3(wr)
    wr3 = jnp.concatenate([wr_hi, wr_lo, wr_hi], axis=0)
    br = jnp.concatenate([b_router_group[0], b_router_expert[0],
                          jnp.zeros((LANES - N_GROUPS - N_EXPERTS,), F32)]).reshape(1, LANES)
    g2n = row(norm2_g[0])
    x1, h2, route = _resid_norm_router(xp2, proj, mod, g2n, wr3, br, nt=seq // 128, tr=128, gate_blk=2, sc_blk=4,
                                       sh_blk=3, proj_row0=0, intos=[None] * 3, total_rows=n,
                                       name="norm2_prompt", **p_mod)
    x1, h2, route = _resid_norm_router(xs2, proj, mod, g2n, wr3, br, gate_blk=2, sc_blk=4, sh_blk=3,
                                       proj_row0=n_p, intos=[x1, h2, route], total_rows=n,
                                       name="norm2_sample", **s_mod)

    e1 = route[:, 2].astype(jnp.int32)
    e2 = route[:, 3].astype(jnp.int32)
    n_tiles = -(-(2 * n + N_EXPERTS * (MOE_TILE - 1)) // MOE_TILE)
    pos, src, steps = _moe_plan(e1, e2, n_tiles)
    y_sorted = _moe_experts(h2, src, w_gate_e[0], w_up_e[0], w_down_e[0], steps, n_tiles)

    fg = row(final_g)
    y_prompt = _final(x1, y_sorted, pos, route, mod, fg, nt=seq // 128, tr=128, gate_blk=5, row0=0, out_rows=n_p,
                      name="final_prompt", **p_mod)
    y_sample = _final(x1, y_sorted, pos, route, mod, fg, gate_blk=5, row0=n_p, out_rows=ns, name="final_sample",
                      **s_mod)

    return (y_prompt.reshape(nb, seq, d), y_sample.reshape(ns, 1, d),
            hg_p[None], conv_p[None], ssm_p.reshape(1, nb, n_ssm_heads, SSM_HEADDIM, SSM_STATE),
            hg_s[None], conv_s[None], ssm_s[None])
```

```python
import functools

import numpy as np
import jax
import jax.numpy as jnp
from jax import lax
from jax.experimental import pallas as pl
from jax.experimental.pallas import tpu as pltpu

F32 = jnp.float32
BF16 = jnp.bfloat16
EPS = 1e-6
LOG2E = 1.4426950408889634

VMEM_LIMIT_BYTES = 56 * 1024 * 1024
LANES = 128

HG_DK = 128
HG_CHUNK = 128
HG_HEADS_PER_STEP = 16
SSM_HEADDIM = 64
SSM_STATE = 128
SSM_HPG = 16
SSM_CHUNK = 128
SSM_CONV = 4
N_GROUPS = 4
EXPERTS_PER_GROUP = 4
N_EXPERTS = 16
MOE_TILE = 128
MOE_UP_VMEM_BYTES = 60 * 1024 * 1024
SLAB_PITCH = 40


def _params(sem, vmem=VMEM_LIMIT_BYTES, flags=None):
    return pltpu.CompilerParams(dimension_semantics=sem, vmem_limit_bytes=vmem, flags=flags)


def _sigmoid(x):
    return 1.0 / (1.0 + jnp.exp(-x))


def _silu(x):
    return x * _sigmoid(x)


def _softplus(x):
    return jnp.maximum(x, 0.0) + jnp.log(1.0 + jnp.exp(-jnp.abs(x)))


def _split3(x):
    p1 = x.astype(BF16)
    r1 = x - p1.astype(F32)
    p2 = r1.astype(BF16)
    r2 = r1 - p2.astype(F32)
    return p1, p2, r2.astype(BF16)


def _dot(a, b):
    return jnp.dot(a, b, preferred_element_type=F32)


def _dot_nt(a, b):
    return lax.dot_general(a, b, (((1,), (1,)), ((), ())), preferred_element_type=F32)


def _dot_tn(a, b):
    return lax.dot_general(a, b, (((0,), (0,)), ((), ())), preferred_element_type=F32)


def _exact_dot(w3_bf16, x_f32):
    return _dot(w3_bf16, jnp.concatenate(_split3(x_f32), axis=0))


def _exact_dot_rhs(x_f32, w3_bf16):
    return _dot(jnp.concatenate(_split3(x_f32), axis=-1), w3_bf16)


def _thrice(w, axis):
    return jnp.concatenate([w, w, w], axis=axis)


def _pcall(body, *, grid, in_specs, args, outs, sem, name, scratch=(), num_prefetch=0, flags=None):
    n_in = len(args) - num_prefetch
    into = [(i, o[3]) for i, o in enumerate(outs) if o[3] is not None]
    in_specs = list(in_specs) + [pl.BlockSpec(memory_space=pl.ANY)] * len(into)
    all_args = list(args) + [buf for _, buf in into]
    aliases = {len(args) + j: i for j, (i, _) in enumerate(into)}

    def wrapped(*refs):
        keep = refs[:num_prefetch + n_in] + refs[num_prefetch + n_in + len(into):]
        return body(*keep)

    grid_spec = pltpu.PrefetchScalarGridSpec(
        num_scalar_prefetch=num_prefetch, grid=grid, in_specs=in_specs,
        out_specs=[o[2] for o in outs], scratch_shapes=list(scratch))
    res = pl.pallas_call(
        wrapped, grid_spec=grid_spec,
        out_shape=[jax.ShapeDtypeStruct(o[0], o[1]) for o in outs],
        input_output_aliases=aliases, compiler_params=_params(sem, flags=flags), name=name,
    )(*all_args)
    return res


def _mm_body(a_ref, b_ref, *rest, n_extra, a_fn, epilogue):
    extras = rest[:n_extra]
    o_ref = rest[n_extra]
    a = a_ref[...]
    if a_fn is not None:
        a = a_fn(a)
    acc = _dot(a.astype(BF16), b_ref[...].astype(BF16))
    if epilogue is not None:
        acc = epilogue(acc, *[e[...] for e in extras])
    o_ref[...] = acc.astype(o_ref.dtype)


def _mm(a, b, *, tm, tn, name, k=None, a_kblk=0, b_kblk=0, col0=0, ncols=None, extras=(), a_fn=None,
        epilogue=None, out_dtype=F32):
    m = a.shape[0]
    k = a.shape[1] if k is None else k
    ncols = b.shape[1] if ncols is None else ncols
    assert m % tm == 0 and ncols % tn == 0 and col0 % tn == 0
    cb = col0 // tn
    in_specs = [pl.BlockSpec((tm, k), lambda i, j: (i, a_kblk), pipeline_mode=pl.Buffered(1)),
                pl.BlockSpec((k, tn), lambda i, j: (b_kblk, j + cb))]
    in_specs += [s for _, s in extras]
    return pl.pallas_call(
        functools.partial(_mm_body, n_extra=len(extras), a_fn=a_fn, epilogue=epilogue),
        grid=(m // tm, ncols // tn),
        in_specs=in_specs,
        out_specs=pl.BlockSpec((tm, tn), lambda i, j: (i, j)),
        out_shape=jax.ShapeDtypeStruct((m, ncols), out_dtype),
        compiler_params=_params(("parallel", "arbitrary")),
        name=name,
    )(a, b, *[x for x, _ in extras])


BRANCH_TM = 832


def _branch_merge_body(a_ref, b_ref, wa_ref, wb_ref, ga0, ga1, gb0, gb1, gl, o_ref):
    ta = _dot(a_ref[...], wa_ref[...].astype(BF16))
    tb = _dot(b_ref[...], wb_ref[...].astype(BF16))
    g_a = jnp.concatenate([ga0[...], ga1[...]], axis=-1)
    last = pl.program_id(1) == pl.num_programs(1) - 1
    g_b = jnp.concatenate([gb0[...], jnp.where(last, gl[...], gb1[...])], axis=-1)
    o_ref[...] = (_sigmoid(g_a) * ta + _sigmoid(g_b) * tb).astype(o_ref.dtype)


def _branch_merge(o_a, y_b, w_a, w_b, u_t, u_l):
    n, d = o_a.shape[0], w_a.shape[1]
    tm, tn = BRANCH_TM, MM_TN
    assert n % tm == 0 and d % tn == 0 and tn == 2 * LANES
    last_blk = u_t.shape[1] // LANES - 1
    half = lambda blk0: pl.BlockSpec((tm, LANES), lambda i, j: (i, jnp.minimum(blk0 + 2 * j, last_blk)))
    gb0 = 1 + d // LANES
    once = pl.Buffered(1)
    return pl.pallas_call(
        _branch_merge_body,
        grid=(n // tm, d // tn),
        in_specs=[pl.BlockSpec((tm, o_a.shape[1]), lambda i, j: (i, 0), pipeline_mode=once),
                  pl.BlockSpec((tm, y_b.shape[1]), lambda i, j: (i, 0), pipeline_mode=once),
                  pl.BlockSpec((w_a.shape[0], tn), lambda i, j: (0, j)),
                  pl.BlockSpec((w_b.shape[0], tn), lambda i, j: (0, j)),
                  half(1), half(2), half(gb0), half(gb0 + 1),
                  pl.BlockSpec((tm, LANES), lambda i, j: (i, 0))],
        out_specs=pl.BlockSpec((tm, tn), lambda i, j: (i, j)),
        out_shape=jax.ShapeDtypeStruct((n, d), BF16),
        compiler_params=_params(("parallel", "arbitrary")),
        name="branch_merge",
    )(o_a, y_b, w_a, w_b, u_t, u_t, u_t, u_t, u_l)


def _rms(x, g):
    return x * lax.rsqrt(jnp.mean(x * x, axis=-1, keepdims=True) + EPS) * g


def _mod_view(mod, per_row):
    return mod if per_row else mod.reshape(mod.shape[0], 1, mod.shape[1])


def _mod_spec(per_row, tr, d, col_blk, nt, row0):
    if per_row:
        rb = row0 // tr
        return pl.BlockSpec((tr, d), lambda b, t: (rb + t, col_blk))
    return pl.BlockSpec((1, 1, d), lambda b, t: (row0 + b, 0, col_blk))


def _rows_spec(tr, d, nt, row0):
    rb = row0 // tr
    return pl.BlockSpec((tr, d), lambda b, t: (rb + b * nt + t, 0))


def _norm_mod_body(x_ref, sc_ref, sh_ref, g_ref, o_ref):
    d = x_ref.shape[-1]
    sc = sc_ref[...].reshape(-1, d)
    sh = sh_ref[...].reshape(-1, d)
    o_ref[...] = (_rms(x_ref[...], g_ref[...]) * (1.0 + sc) + sh).astype(o_ref.dtype)


def _norm_mod(x, mod, g, *, nb, nt, tr, per_row, mod_row0, sc_blk, sh_blk, into, total_rows, out_row0, name):
    d = x.shape[-1]
    mv = _mod_view(mod, per_row)
    return _pcall(
        _norm_mod_body, grid=(nb, nt),
        in_specs=[_rows_spec(tr, d, nt, 0),
                  _mod_spec(per_row, tr, d, sc_blk, nt, mod_row0),
                  _mod_spec(per_row, tr, d, sh_blk, nt, mod_row0),
                  pl.BlockSpec((1, d), lambda b, t: (0, 0))],
        args=[x, mv, mv, g],
        outs=[((total_rows, d), BF16, _rows_spec(tr, d, nt, out_row0), into)],
        sem=("parallel", "arbitrary"), name=name)[0]


def _router_math(lg):
    lane = lax.broadcasted_iota(jnp.int32, lg.shape, 1)
    neg = jnp.float32(-jnp.inf)
    gm = jnp.where(lane < N_GROUPS, lg, neg)
    gmax = jnp.max(gm, axis=-1, keepdims=True)
    gidx = jnp.min(jnp.where(gm == gmax, lane, LANES), axis=-1, keepdims=True)
    gsum = jnp.sum(jnp.where(lane < N_GROUPS, jnp.exp(gm - gmax), 0.0), axis=-1, keepdims=True)
    p_group = 1.0 / gsum
    lo = N_GROUPS + EXPERTS_PER_GROUP * gidx
    em = jnp.where((lane >= lo) & (lane < lo + EXPERTS_PER_GROUP), lg, neg)
    m1 = jnp.max(em, axis=-1, keepdims=True)
    i1 = jnp.min(jnp.where(em == m1, lane, LANES), axis=-1, keepdims=True)
    em2 = jnp.where(lane == i1, neg, em)
    m2 = jnp.max(em2, axis=-1, keepdims=True)
    i2 = jnp.min(jnp.where(em2 == m2, lane, LANES), axis=-1, keepdims=True)
    r = jnp.exp(m2 - m1)
    w1 = p_group / (1.0 + r)
    w2 = p_group * r / (1.0 + r)
    e1 = (i1 - N_GROUPS).astype(F32)
    e2 = (i2 - N_GROUPS).astype(F32)
    out = jnp.where(lane == 0, w1, 0.0)
    out = jnp.where(lane == 1, w2, out)
    out = jnp.where(lane == 2, e1, out)
    out = jnp.where(lane == 3, e2, out)
    return out


def _resid_norm_router_body(x_ref, p_ref, gt_ref, sc_ref, sh_ref, g_ref, wr_ref, br_ref, x1_ref, h_ref, rt_ref):
    d = x_ref.shape[-1]
    gt = gt_ref[...].reshape(-1, d)
    sc = sc_ref[...].reshape(-1, d)
    sh = sh_ref[...].reshape(-1, d)
    x1 = x_ref[...] + gt * p_ref[...]
    x1_ref[...] = x1
    h = _rms(x1, g_ref[...]) * (1.0 + sc) + sh
    tr = x_ref.shape[0]
    for k in range(SLAB_PITCH):
        slab = h[:, k * LANES:(k + 1) * LANES] if k < d // LANES else jnp.zeros((tr, LANES), F32)
        h_ref[pl.ds(k, tr, stride=SLAB_PITCH), :] = slab
    h1, h2, _ = _split3(h)
    lg = _dot(jnp.concatenate([h1, h1, h2], axis=-1), wr_ref[...])
    rt_ref[...] = _router_math(lg + br_ref[...])


def _resid_norm_router(x, proj, mod, g, wr3, br, *, nb, nt, tr, per_row, mod_row0, gate_blk, sc_blk, sh_blk,
                       proj_row0, intos, total_rows, name):
    d = x.shape[-1]
    mv = _mod_view(mod, per_row)
    ms = lambda blk: _mod_spec(per_row, tr, d, blk, nt, mod_row0)
    orow = _rows_spec(tr, d, nt, proj_row0)
    return _pcall(
        _resid_norm_router_body, grid=(nb, nt),
        in_specs=[_rows_spec(tr, d, nt, 0), orow, ms(gate_blk), ms(sc_blk), ms(sh_blk),
                  pl.BlockSpec((1, d), lambda b, t: (0, 0)),
                  pl.BlockSpec((3 * d, LANES), lambda b, t: (0, 0)),
                  pl.BlockSpec((1, LANES), lambda b, t: (0, 0))],
        args=[x, proj, mv, mv, mv, g, wr3, br],
        outs=[((total_rows, d), F32, orow, intos[0]),
              ((total_rows * SLAB_PITCH, LANES), F32, _rows_spec(tr * SLAB_PITCH, LANES, nt, proj_row0 * SLAB_PITCH),
               intos[1]),
              ((total_rows, LANES), F32, _rows_spec(tr, LANES, nt, proj_row0), intos[2])],
        sem=("parallel", "arbitrary"), name=name)


def _row_gather_start(src_hbm, idx_ref, idx0, dst, sem, n, unrolled=False, rows=1, pitch=1):
    def body(r, carry):
        s0 = idx_ref[idx0 + r] * pitch
        d0 = r * pitch
        if pitch % 8 == 0:
            s0 = pl.multiple_of(s0, 8)
            d0 = d0 if isinstance(d0, int) else pl.multiple_of(d0, 8)
        pltpu.make_async_copy(src_hbm.at[pl.ds(s0, rows), :], dst.at[pl.ds(d0, rows), :], sem).start()
        return carry

    if unrolled:
        for r in range(n):
            body(r, 0)
    else:
        lax.fori_loop(0, n, body, 0, unroll=8)


def _row_gather_wait(src_hbm, dst, sem, n, rows=1, pitch=1):
    def body(r, carry):
        d0 = pl.multiple_of(r * pitch, 8) if pitch % 8 == 0 else r * pitch
        pltpu.make_async_copy(src_hbm.at[pl.ds(0, rows), :], dst.at[pl.ds(d0, rows), :], sem).wait()
        return carry

    lax.fori_loop(0, n, body, 0, unroll=8)


def _final_body(pos_ref, x1_ref, rt_ref, gt_ref, g_ref, y_hbm, o_ref, ybuf, sem, *, tr, tok0, n_tok):
    d = x1_ref.shape[-1]
    i = pl.program_id(0)
    slot = lax.rem(i, 2)

    def start(step, sl):
        for k in range(2):
            _row_gather_start(y_hbm, pos_ref, k * n_tok + tok0 + step * tr, ybuf.at[sl, k], sem.at[sl], tr)

    @pl.when(i == 0)
    def _():
        start(0, 0)

    @pl.when(i + 1 < pl.num_programs(0))
    def _():
        start(i + 1, 1 - slot)

    for k in range(2):
        _row_gather_wait(y_hbm, ybuf.at[slot, k], sem.at[slot], tr)
    gt = gt_ref[...].reshape(-1, d)
    rt = rt_ref[...]
    moe = rt[:, 0:1] * ybuf[slot, 0] + rt[:, 1:2] * ybuf[slot, 1]
    o_ref[...] = _rms(x1_ref[...] + gt * moe, g_ref[...])


def _final(x1, y_sorted, pos, route, mod, g, *, nb, nt, tr, per_row, mod_row0, gate_blk, row0, out_rows, name):
    d = x1.shape[-1]
    n_tok = x1.shape[0]
    rb = row0 // tr
    if per_row:
        mspec = pl.BlockSpec((tr, d), lambda i, pos_: (mod_row0 // tr + i, gate_blk))
    else:
        mspec = pl.BlockSpec((1, 1, d), lambda i, pos_: (mod_row0 + i // nt, 0, gate_blk))
    return _pcall(
        functools.partial(_final_body, tr=tr, tok0=row0, n_tok=n_tok), grid=(nb * nt,),
        in_specs=[pl.BlockSpec((tr, d), lambda i, pos_: (rb + i, 0)),
                  pl.BlockSpec((tr, LANES), lambda i, pos_: (rb + i, 0)),
                  mspec,
                  pl.BlockSpec((1, d), lambda i, pos_: (0, 0)),
                  pl.BlockSpec(memory_space=pl.ANY)],
        args=[pos, x1, route, _mod_view(mod, per_row), g, y_sorted],
        outs=[((out_rows, d), F32, pl.BlockSpec((tr, d), lambda i, pos_: (i, 0)), None)],
        sem=("arbitrary",), name=name, num_prefetch=1,
        scratch=[pltpu.VMEM((2, 2, tr, d), F32), pltpu.SemaphoreType.DMA((2,))])[0]


HG_MXU_LEVELS = 3


def _hgrn_tables(c):
    nlev = int(np.log2(c))
    t = np.arange(c)[:, None]
    u = np.arange(c)[None, :]
    mats = [u <= t]
    masks = []
    for l in range(nlev):
        m = 1 << l
        p = (t // (2 * m)) * (2 * m) + m - 1
        upper = ((t >> l) & 1) == 1
        if l < HG_MXU_LEVELS:
            mats.append(np.where(upper, (u > p) & (u <= t), (u > t) & (u <= p)))
        same = (t >> (l + 1)) == (u >> (l + 1))
        masks.append(same & upper & (((u >> l) & 1) == 0))
    masks.append(t == u)
    w = np.concatenate([m.astype(np.float32) for m in mats], axis=0)
    return jnp.asarray(np.tile(w, (1, 3)), BF16), jnp.asarray(np.stack(masks).astype(np.float32))


def _hgrn_prompt_body(q_ref, f_ref, v_ref, og_ref, lb_ref, gn_ref, w_ref, mk_ref, o_ref, s_ref, st_scr, *,
                      nchunks, nlev):
    c = HG_CHUNK
    hp = st_scr.shape[0]
    t = pl.program_id(2)

    @pl.when(t == 0)
    def _():
        st_scr[...] = jnp.zeros_like(st_scr)

    lb = lb_ref[...]
    gn = gn_ref[...]
    trow = lax.broadcasted_iota(jnp.int32, (c, HG_DK), 0)
    upper = [((trow >> l) & 1) == 1 for l in range(HG_MXU_LEVELS, nlev)]

    def chunk(ci, carry):
        r0 = pl.multiple_of(ci * c, c)
        rows = pl.ds(r0, c)
        lns = [slice(i * HG_DK, (i + 1) * HG_DK) for i in range(hp)]
        f = [lb[:, ln] + (1.0 - lb[:, ln]) * _sigmoid(f_ref[rows, ln]) for ln in lns]
        g = [_exact_dot(w_ref[...], jnp.log2(fi)) for fi in f]
        kk = [1.0 - fi for fi in f]
        q = [_silu(q_ref[rows, ln]) for ln in lns]
        v = [v_ref[rows, ln].astype(BF16) for ln in lns]
        b = [gi[0:c] for gi in g]
        st = [st_scr[i] for i in range(hp)]
        qb = [qi.astype(BF16) for qi in q]
        kb = [ki.astype(BF16) for ki in kk]
        o = [_dot_nt(qb[i] * jnp.exp2(b[i]).astype(BF16), st[i].astype(BF16)) for i in range(hp)]
        scores = [mk_ref[nlev] * _dot_nt(qb[i], kb[i]) for i in range(hp)]
        for l in range(nlev):
            for i in range(hp):
                if l < HG_MXU_LEVELS:
                    x = jnp.exp2(g[i][(1 + l) * c:(2 + l) * c])
                else:
                    m = 1 << l
                    b3 = b[i].reshape(c // (2 * m), 2 * m, HG_DK)
                    d = (b3 - b3[:, m - 1:m, :]).reshape(c, HG_DK)
                    x = jnp.exp2(jnp.where(upper[l - HG_MXU_LEVELS], d, -d))
                x = x.astype(BF16)
                scores[i] = scores[i] + mk_ref[l] * _dot_nt(qb[i] * x, kb[i] * x)
        for i, ln in enumerate(lns):
            b_end = b[i][c - 1:c, :]
            oi = o[i] + _dot(scores[i].astype(BF16), v[i])
            kw = kb[i] * jnp.exp2(b_end - b[i]).astype(BF16)
            st_scr[i] = st[i] * jnp.exp2(b_end) + _dot_tn(v[i], kw)
            oi = _rms(oi, gn[:, ln]) * _silu(og_ref[rows, ln])
            o_ref[rows, ln] = oi.astype(o_ref.dtype)
        return carry

    lax.fori_loop(0, nchunks, chunk, 0)

    @pl.when(t == pl.num_programs(2) - 1)
    def _():
        for i in range(hp):
            s_ref[0, i] = st_scr[i].T


def _hgrn_prompt(u, lb, gn, *, nb, seq, nheads, col_q, col_f, col_v, col_og, total_rows, tb):
    c = HG_CHUNK
    nt = seq // tb
    nlev = int(np.log2(c))
    w, mk = _hgrn_tables(c)
    hp = HG_HEADS_PER_STEP
    wd = hp * HG_DK
    cs = lambda col: pl.BlockSpec((tb, wd), lambda b, h, t: (b * nt + t, col // wd + h))
    hv = pl.BlockSpec((1, wd), lambda b, h, t: (0, h))
    return _pcall(
        functools.partial(_hgrn_prompt_body, nchunks=tb // c, nlev=nlev),
        grid=(nb, nheads // hp, nt),
        in_specs=[cs(col_q), cs(col_f), cs(col_v), cs(col_og), hv, hv,
                  pl.BlockSpec(w.shape, lambda b, h, t: (0, 0)),
                  pl.BlockSpec(mk.shape, lambda b, h, t: (0, 0, 0))],
        args=[u, u, u, u, lb, gn, w, mk],
        outs=[((total_rows, nheads * HG_DK), BF16, pl.BlockSpec((tb, wd), lambda b, h, t: (b * nt + t, h)), None),
              ((nb, nheads, HG_DK, HG_DK), F32, pl.BlockSpec((1, hp, HG_DK, HG_DK), lambda b, h, t: (b, h, 0, 0)), None)],
        sem=("parallel", "parallel", "arbitrary"), name="hgrn_prompt",
        scratch=[pltpu.VMEM((hp, HG_DK, HG_DK), F32)])


def _ssm_expand_tables(n_heads):
    ng = n_heads // SSM_HPG
    w2, w1 = SSM_HPG * LANES, SSM_HPG * SSM_HEADDIM
    e = np.zeros((ng, n_heads, w2 + w1), np.float32)
    for g in range(ng):
        for i in range(SSM_HPG):
            e[g, g * SSM_HPG + i, i * LANES:(i + 1) * LANES] = 1.0
            e[g, g * SSM_HPG + i, w2 + i * SSM_HEADDIM:w2 + (i + 1) * SSM_HEADDIM] = 1.0
    return jnp.asarray(np.tile(e, (1, 3, 1)), BF16)


def _conv_silu(buf_ref, w, bias, r0, n):
    b = buf_ref[pl.ds(r0, n + 8), :]
    acc = b * w[0:1, :]
    for j in range(1, SSM_CONV):
        acc = pltpu.roll(acc, 1, axis=0) + b * w[j:j + 1, :]
    return _silu(acc[8:, :] + bias)


def _ssd_prompt_body(x_ref, b_ref, c_ref, z_ref, dt_ref, wx_ref, wb_ref, wc_ref, bx_ref, bb_ref, bc_ref,
                     dtb_ref, alog_ref, dsk_ref, e_ref, gn_ref, y_ref, s_ref,
                     ht_scr, cbx, cbb, cbc, *, nchunks):
    L = SSM_CHUNK
    tb = nchunks * L
    w2 = SSM_HPG * LANES
    t = pl.program_id(2)

    @pl.when(t == 0)
    def _():
        ht_scr[...] = jnp.zeros_like(ht_scr)
        cbx[0:8, :] = jnp.zeros((8, cbx.shape[1]), F32)
        cbb[0:8, :] = jnp.zeros((8, cbb.shape[1]), F32)
        cbc[0:8, :] = jnp.zeros((8, cbc.shape[1]), F32)

    cbx[8:8 + tb, :] = x_ref[...]
    cbb[8:8 + tb, :] = b_ref[...]
    cbc[8:8 + tb, :] = c_ref[...]

    e_all = e_ref[0]
    e1 = e_all[:, w2:]
    neg_a = -jnp.exp(alog_ref[...]) * LOG2E
    dsk1 = _exact_dot_rhs(dsk_ref[...], e1)
    row = lax.broadcasted_iota(jnp.int32, (L, L), 0)
    col = lax.broadcasted_iota(jnp.int32, (L, L), 1)
    causal = col <= row
    tril = _thrice(jnp.where(causal, 1.0, 0.0).astype(BF16), 1)
    lane = lax.broadcasted_iota(jnp.int32, (L, LANES), 1)
    lo_half = lane < SSM_HEADDIM

    for ci in range(nchunks):
        r0 = ci * L
        xs = _conv_silu(cbx, wx_ref[...], bx_ref[...], r0, L)
        bm = _conv_silu(cbb, wb_ref[...], bb_ref[...], r0, L).astype(BF16)
        cm = _conv_silu(cbc, wc_ref[...], bc_ref[...], r0, L).astype(BF16)
        dt = _softplus(dt_ref[r0:r0 + L, :] + dtb_ref[...])
        cum = _exact_dot(tril, dt * neg_a)
        cum_e = _exact_dot_rhs(cum, e_all)
        cum1 = cum_e[:, w2:]
        dt1 = _exact_dot_rhs(dt, e1)
        xdt = xs * dt1
        cum_end = cum1[L - 1:L, :]
        ht = ht_scr[...]
        y = _dot(cm, ht.astype(BF16)) * jnp.exp2(cum1)
        cb = _dot_nt(cm, bm)
        parts = []
        for j in range(SSM_HPG // 2):
            xp = xdt[:, j * LANES:(j + 1) * LANES]
            acc = None
            for half in range(2):
                i = 2 * j + half
                colb = cum_e[:, i * LANES:(i + 1) * LANES]
                seg = colb - colb.T
                m = (cb * jnp.exp2(jnp.where(causal, seg, -jnp.inf))).astype(BF16)
                xh = jnp.where(lo_half if half == 0 else ~lo_half, xp, 0.0).astype(BF16)
                d = _dot(m, xh)
                acc = d if acc is None else acc + d
            parts.append(acc)
        y = y + jnp.concatenate(parts, axis=-1) + dsk1 * xs
        y = y * _silu(z_ref[r0:r0 + L, :])
        y_ref[r0:r0 + L, :] = _rms(y, gn_ref[...]).astype(y_ref.dtype)
        xw = (xdt * jnp.exp2(cum_end - cum1)).astype(BF16)
        ht_scr[...] = ht * jnp.exp2(cum_end) + _dot_tn(bm, xw)

    cbx[0:8, :] = cbx[tb:tb + 8, :]
    cbb[0:8, :] = cbb[tb:tb + 8, :]
    cbc[0:8, :] = cbc[tb:tb + 8, :]

    @pl.when(t == pl.num_programs(2) - 1)
    def _():
        s_ref[0, 0] = ht_scr[...].T


def _ssd_prompt(u, u_dt, conv_w, conv_b, dt_bias, a_log, d_skip, gn, *, nb, seq, col_z, col_x, total_rows, tb):
    n_heads = u_dt.shape[1]
    ng = n_heads // SSM_HPG
    w1 = SSM_HPG * SSM_HEADDIM
    inner = ng * w1
    nt = seq // tb
    e = _ssm_expand_tables(n_heads)
    xb, bb, cb_, zb = col_x // w1, (col_x + inner) // LANES, (col_x + inner + ng * SSM_STATE) // LANES, col_z // w1
    wide = lambda blk0: pl.BlockSpec((tb, w1), lambda b, g, t: (b * nt + t, blk0 + g))
    thin = lambda blk0: pl.BlockSpec((tb, LANES), lambda b, g, t: (b * nt + t, blk0 + g))
    cw = lambda rows, width, blk0: pl.BlockSpec((rows, width), lambda b, g, t: (0, blk0 + g))
    full = lambda a: pl.BlockSpec(a.shape, lambda b, g, t: (0,) * a.ndim)
    return _pcall(
        functools.partial(_ssd_prompt_body, nchunks=tb // SSM_CHUNK),
        grid=(nb, ng, nt),
        in_specs=[wide(xb), thin(bb), thin(cb_), wide(zb),
                  pl.BlockSpec((tb, n_heads), lambda b, g, t: (b * nt + t, 0)),
                  cw(SSM_CONV, w1, 0), cw(SSM_CONV, LANES, inner // LANES),
                  cw(SSM_CONV, LANES, (inner + ng * SSM_STATE) // LANES),
                  cw(1, w1, 0), cw(1, LANES, inner // LANES), cw(1, LANES, (inner + ng * SSM_STATE) // LANES),
                  full(dt_bias), full(a_log), full(d_skip),
                  pl.BlockSpec((1,) + e.shape[1:], lambda b, g, t: (g, 0, 0)),
                  cw(1, w1, 0)],
        args=[u, u, u, u, u_dt, conv_w, conv_w, conv_w, conv_b, conv_b, conv_b, dt_bias, a_log, d_skip, e, gn],
        outs=[((total_rows, inner), BF16, pl.BlockSpec((tb, w1), lambda b, g, t: (b * nt + t, g)), None),
              ((nb, ng, w1, SSM_STATE), F32, pl.BlockSpec((1, 1, w1, SSM_STATE), lambda b, g, t: (b, g, 0, 0)), None)],
        sem=("parallel", "parallel", "arbitrary"), name="ssd_prompt",
        scratch=[pltpu.VMEM((SSM_STATE, w1), F32), pltpu.VMEM((tb + 8, w1), F32),
                 pltpu.VMEM((tb + 8, LANES), F32), pltpu.VMEM((tb + 8, LANES), F32)])


STEP_TOKENS = 16
STEP_HEADS = 4


def _hgrn_step_body(q_ref, v_ref, og_ref, ft_ref, lbc_ref, gn_ref, s_ref, o_ref, so_ref):
    lbc = lbc_ref[...]
    f = lbc + (1.0 - lbc) * _sigmoid(ft_ref[0])
    q = _silu(q_ref[...])
    v = v_ref[...]
    gn = gn_ref[...]
    og = og_ref[...]
    for i in range(STEP_HEADS):
        ln = slice(i * HG_DK, (i + 1) * HG_DK)
        rows = []
        for j in range(STEP_TOKENS):
            vj = v[j:j + 1, ln]
            sn = vj + f[ln, j:j + 1] * (s_ref[j, i] - vj)
            so_ref[j, i] = sn
            qj = jnp.broadcast_to(q[j:j + 1, ln], (16, HG_DK)).astype(BF16)
            rows.append(_dot(qj, sn.astype(BF16))[0:1, :])
        o = jnp.concatenate(rows, axis=0)
        o_ref[:, ln] = (_rms(o, gn[:, ln]) * _silu(og[:, ln])).astype(o_ref.dtype)


def _hgrn_step(u, f_t, lb_col, gn, state, *, row0, nheads, col_q, col_v, col_og, into):
    n = state.shape[0]
    tb = STEP_TOKENS
    hp = STEP_HEADS
    wd = hp * HG_DK
    rb = row0 // tb
    cs = lambda col: pl.BlockSpec((tb, wd), lambda b, h: (rb + b, col // wd + h))
    sspec = pl.BlockSpec((tb, hp, HG_DK, HG_DK), lambda b, h: (b, h, 0, 0))
    return _pcall(
        _hgrn_step_body, grid=(n // tb, nheads // hp),
        in_specs=[cs(col_q), cs(col_v), cs(col_og),
                  pl.BlockSpec((1, wd, tb), lambda b, h: (b, h, 0)),
                  pl.BlockSpec((wd, 1), lambda b, h: (h, 0)),
                  pl.BlockSpec((1, wd), lambda b, h: (0, h)),
                  sspec],
        args=[u, u, u, f_t, lb_col, gn, state],
        outs=[(into.shape, into.dtype, pl.BlockSpec((tb, wd), lambda b, h: (rb + b, h)), into),
              (state.shape, F32, sspec, None)],
        sem=("parallel", "arbitrary"), name="hgrn_step")


def _ssd_prep_body(x_ref, b_ref, c_ref, sx0, sx1, sx2, sb0, sb1, sb2, sc0, sc1, sc2, dt_ref,
                   wx_ref, wb_ref, wc_ref, bx_ref, bb_ref, bc_ref, dtb_ref, alog_ref, e_ref,
                   xs_ref, bm_ref, cm_ref, xdt_ref, da_ref):
    def conv(taps, cur, w_ref, bias_ref):
        w = w_ref[...]
        acc = bias_ref[...] + cur[...] * w[SSM_CONV - 1:SSM_CONV, :]
        for j, tap in enumerate(taps):
            acc = acc + tap[...] * w[j:j + 1, :]
        return _silu(acc)

    xs = conv((sx0, sx1, sx2), x_ref, wx_ref, bx_ref)
    xs_ref[...] = xs
    bm_ref[...] = conv((sb0, sb1, sb2), b_ref, wb_ref, bb_ref)
    cm_ref[...] = conv((sc0, sc1, sc2), c_ref, wc_ref, bc_ref)
    e1 = e_ref[0][:, SSM_HPG * LANES:]
    dt = _softplus(dt_ref[...] + dtb_ref[...])
    xdt_ref[...] = xs * _exact_dot_rhs(dt, e1)
    da_ref[...] = _exact_dot_rhs(jnp.exp(dt * -jnp.exp(alog_ref[...])), e1)


def _ssd_prep(u, u_dt, conv_state2d, conv_w, conv_b, dt_bias, a_log, *, row0, n, col_x):
    n_heads = u_dt.shape[1]
    ng = n_heads // SSM_HPG
    w1 = SSM_HPG * SSM_HEADDIM
    inner = ng * w1
    cdim = inner + 2 * ng * SSM_STATE
    e = _ssm_expand_tables(n_heads)
    rb = row0 // n
    xoff, boff, coff = 0, inner, inner + ng * SSM_STATE
    cur = lambda width, off: pl.BlockSpec((n, width), lambda g: (rb, (col_x + off) // width + g))
    tap = lambda width, off, j: pl.BlockSpec((n, width), lambda g: (0, (j * cdim + off) // width + g))
    cw = lambda rows, width, off: pl.BlockSpec((rows, width), lambda g: (0, off // width + g))
    full = lambda a: pl.BlockSpec(a.shape, lambda g: (0,) * a.ndim)
    taps = [tap(w1, xoff, j) for j in range(3)] + [tap(LANES, boff, j) for j in range(3)] + \
           [tap(LANES, coff, j) for j in range(3)]
    o_wide = pl.BlockSpec((n, w1), lambda g: (0, g))
    o_thin = pl.BlockSpec((n, LANES), lambda g: (0, g))
    return _pcall(
        _ssd_prep_body, grid=(ng,),
        in_specs=[cur(w1, xoff), cur(LANES, boff), cur(LANES, coff)] + taps +
                 [pl.BlockSpec((n, n_heads), lambda g: (rb, 0)),
                  cw(SSM_CONV, w1, xoff), cw(SSM_CONV, LANES, boff), cw(SSM_CONV, LANES, coff),
                  cw(1, w1, xoff), cw(1, LANES, boff), cw(1, LANES, coff),
                  full(dt_bias), full(a_log),
                  pl.BlockSpec((1,) + e.shape[1:], lambda g: (g, 0, 0))],
        args=[u, u, u] + [conv_state2d] * 9 + [u_dt, conv_w, conv_w, conv_w, conv_b, conv_b, conv_b,
                                               dt_bias, a_log, e],
        outs=[((n, inner), F32, o_wide, None), ((n, ng * SSM_STATE), F32, o_thin, None),
              ((n, ng * SSM_STATE), F32, o_thin, None), ((n, inner), F32, o_wide, None),
              ((n, inner), F32, o_wide, None)],
        sem=("arbitrary",), name="ssd_prep")


def _ssd_step_body(xs_ref, bm_ref, cm_ref, z_ref, xdt_ref, da_ref, dsk_ref, gn_ref, s_ref, y_ref, so_ref):
    w1 = SSM_HPG * SSM_HEADDIM
    bm = bm_ref[...]
    cm = cm_ref[...]
    xdt = xdt_ref[0]
    da = da_ref[0]
    rows = []
    for j in range(STEP_TOKENS):
        h = s_ref[j].reshape(w1, SSM_STATE)
        hn = da[:, j:j + 1] * h + xdt[:, j:j + 1] * bm[j:j + 1, :]
        so_ref[j] = hn.reshape(SSM_HPG, SSM_HEADDIM, SSM_STATE)
        cj = jnp.broadcast_to(cm[j:j + 1, :], (16, SSM_STATE)).astype(BF16)
        rows.append(_dot_nt(cj, hn.astype(BF16))[0:1, :])
    y = jnp.concatenate(rows, axis=0) + dsk_ref[...] * xs_ref[...]
    y = y * _silu(z_ref[...])
    y_ref[...] = _rms(y, gn_ref[...]).astype(y_ref.dtype)


def _ssd_step(u, xs, bm, cm, xdt_t, da_t, dsk_full, gn, state, *, row0, col_z, into):
    n, n_heads = state.shape[0], state.shape[1]
    ng = n_heads // SSM_HPG
    w1 = SSM_HPG * SSM_HEADDIM
    tb = STEP_TOKENS
    rb = row0 // tb
    sspec = pl.BlockSpec((tb, SSM_HPG, SSM_HEADDIM, SSM_STATE), lambda b, g: (b, g, 0, 0))
    colspec = pl.BlockSpec((1, w1, tb), lambda b, g: (b, g, 0))
    return _pcall(
        _ssd_step_body, grid=(n // tb, ng),
        in_specs=[pl.BlockSpec((tb, w1), lambda b, g: (b, g)),
                  pl.BlockSpec((tb, SSM_STATE), lambda b, g: (b, g)),
                  pl.BlockSpec((tb, SSM_STATE), lambda b, g: (b, g)),
                  pl.BlockSpec((tb, w1), lambda b, g: (rb + b, col_z // w1 + g)),
                  colspec, colspec,
                  pl.BlockSpec((1, w1), lambda b, g: (0, g)),
                  pl.BlockSpec((1, w1), lambda b, g: (0, g)),
                  sspec],
        args=[xs, bm, cm, u, xdt_t, da_t, dsk_full, gn, state],
        outs=[(into.shape, into.dtype, pl.BlockSpec((tb, w1), lambda b, g: (rb + b, g)), into),
              (state.shape, F32, sspec, None)],
        sem=("parallel", "arbitrary"), name="ssd_step")


def _moe_plan(e1, e2, n_tiles):
    n = e1.shape[0]
    tile = MOE_TILE
    eid = jnp.concatenate([e1, e2]).astype(jnp.int32)
    onehot = (eid[:, None] == jnp.arange(N_EXPERTS, dtype=jnp.int32)[None, :]).astype(jnp.int32)
    blk = 128
    assert (2 * n) % blk == 0
    oh3 = onehot.astype(F32).reshape(2 * n // blk, blk, N_EXPERTS)
    tri = jnp.tril(jnp.ones((blk, blk), F32), -1)
    inblk = jnp.einsum("ts,bse->bte", tri, oh3, preferred_element_type=F32)
    btot = jnp.sum(oh3, axis=1)
    before = (jnp.cumsum(btot, axis=0) - btot)[:, None, :] + inblk
    rank = jnp.sum(before.reshape(2 * n, N_EXPERTS) * onehot, axis=1).astype(jnp.int32)
    counts = jnp.sum(onehot, axis=0)
    ptiles = (counts + tile - 1) // tile
    tstart = jnp.cumsum(ptiles) - ptiles
    pos = (tstart * tile)[eid] + rank
    tok = jnp.concatenate([jnp.arange(n, dtype=jnp.int32)] * 2)
    src = jnp.zeros((n_tiles * tile,), jnp.int32).at[pos].set(tok)
    n_used = jnp.sum(ptiles)
    s = jnp.arange(n_tiles, dtype=jnp.int32)
    tend = jnp.cumsum(ptiles)
    ex = jnp.sum((tend[None, :] <= jnp.minimum(s, n_used - 1)[:, None]).astype(jnp.int32), axis=1)
    active = (s < n_used).astype(jnp.int32)
    first = jnp.concatenate([jnp.ones((1,), jnp.int32), (ex[1:] != ex[:-1]).astype(jnp.int32)])
    nstep = tend[ex]
    has_next = (nstep < n_used).astype(jnp.int32)
    nex = jnp.where(has_next == 1, ex[jnp.minimum(nstep, n_tiles - 1)], ex)
    i32 = lambda a: a.astype(jnp.int32)
    return i32(pos), src, (i32(ex), i32(nex), has_next, first, active)


def _moe_up_body(ex_ref, nex_ref, hasn_ref, first_ref, act_ref, src_ref, x_hbm, wg_hbm, wu_hbm, o_ref,
                 stage_g, stage_u, wg_s, wu_s, xbuf, sem, wsem):
    s = pl.program_id(0)
    last = pl.num_programs(0) - 1
    slot = lax.rem(s, 2)
    tile = o_ref.shape[0]
    n_slab = wg_s.shape[0] // LANES
    slab = dict(rows=n_slab, pitch=SLAB_PITCH)
    nxt = jnp.minimum(s + 1, last)

    def start(step, sl, unrolled=False):
        _row_gather_start(x_hbm, src_ref, step * tile, xbuf.at[sl], sem.at[sl], tile, unrolled, **slab)

    def weight_copies(e):
        return (pltpu.make_async_copy(wg_hbm.at[e], stage_g, wsem.at[0]),
                pltpu.make_async_copy(wu_hbm.at[e], stage_u, wsem.at[1]))

    @pl.when(s == 0)
    def _():
        start(0, 0)
        for cp in weight_copies(ex_ref[0]):
            cp.start(priority=1)

    @pl.when(first_ref[s] == 1)
    def _():
        for cp, stage, dst in zip(weight_copies(ex_ref[s]), (stage_g, stage_u), (wg_s, wu_s)):
            cp.wait()
            dst[...] = stage[...].astype(BF16)

        @pl.when(hasn_ref[s] == 1)
        def _():
            for cp in weight_copies(nex_ref[s]):
                cp.start(priority=1)

    _row_gather_wait(x_hbm, xbuf.at[slot], sem.at[slot], tile, **slab)

    @pl.when(act_ref[s] == 1)
    def _():
        xb = xbuf.at[slot]
        x = jnp.concatenate([xb[pl.ds(k, tile, stride=SLAB_PITCH), :] for k in range(n_slab)], axis=-1)
        x = x.astype(BF16)
        start(nxt, 1 - slot, unrolled=True)
        o_ref[...] = (_silu(_dot(x, wg_s[...])) * _dot(x, wu_s[...])).astype(o_ref.dtype)

    @pl.when(act_ref[s] == 0)
    def _():
        start(nxt, 1 - slot)
        o_ref[...] = jnp.zeros_like(o_ref)

    @pl.when(s == last)
    def _():
        _row_gather_wait(x_hbm, xbuf.at[1 - slot], sem.at[1 - slot], tile, **slab)


def _moe_down_body(ex_ref, nex_ref, hasn_ref, first_ref, act_ref, a_ref, wd_ref, o_ref, wd_s):
    s = pl.program_id(0)

    @pl.when(first_ref[s] == 1)
    def _():
        wd_s[...] = wd_ref[0].astype(BF16)

    @pl.when(act_ref[s] == 1)
    def _():
        o_ref[...] = _dot(a_ref[...], wd_s[...])

    @pl.when(act_ref[s] == 0)
    def _():
        o_ref[...] = jnp.zeros_like(o_ref)


def _moe_experts(x, src, w_gate, w_up, w_down, steps, n_tiles):
    tile = MOE_TILE
    d, de = w_gate.shape[1], w_gate.shape[2]
    rows = n_tiles * tile
    any_spec = pl.BlockSpec(memory_space=pl.ANY)
    act = pl.pallas_call(
        _moe_up_body,
        grid_spec=pltpu.PrefetchScalarGridSpec(
            num_scalar_prefetch=6, grid=(n_tiles,),
            in_specs=[any_spec, any_spec, any_spec],
            out_specs=pl.BlockSpec((tile, de), lambda s, *_: (s, 0)),
            scratch_shapes=[pltpu.VMEM((d, de), F32), pltpu.VMEM((d, de), F32),
                            pltpu.VMEM((d, de), BF16), pltpu.VMEM((d, de), BF16),
                            pltpu.VMEM((2, tile * SLAB_PITCH, LANES), F32),
                            pltpu.SemaphoreType.DMA((2,)), pltpu.SemaphoreType.DMA((2,))]),
        out_shape=jax.ShapeDtypeStruct((rows, de), BF16),
        compiler_params=_params(("arbitrary",), vmem=MOE_UP_VMEM_BYTES), name="moe_up",
    )(*steps, src, x, w_gate, w_up)
    wmap = lambda s, ex, nex, hasn, fi, ac: (jnp.where(fi[s] == 1, ex[s], nex[s]), 0, 0)
    y = pl.pallas_call(
        _moe_down_body,
        grid_spec=pltpu.PrefetchScalarGridSpec(
            num_scalar_prefetch=5, grid=(n_tiles,),
            in_specs=[pl.BlockSpec((tile, de), lambda s, *_: (s, 0)), pl.BlockSpec((1, de, d), wmap)],
            out_specs=pl.BlockSpec((tile, d), lambda s, *_: (s, 0)),
            scratch_shapes=[pltpu.VMEM((de, d), BF16)]),
        out_shape=jax.ShapeDtypeStruct((rows, d), F32),
        compiler_params=_params(("arbitrary",)), name="moe_down",
    )(*steps, act, w_down)
    return y


MM_TM = 1664
MM_TN = 256
MM_TN_WIDE = 512


def kernel(x_prompt, x_sample, state_hgrn, state_conv, state_ssm, c_prompt, c_sample, lb_params, w_mod, b_mod,
           norm1_g, norm2_g, w_in, hg_norm_g, conv_w, conv_b, dt_bias, a_log, d_skip, ssm_norm_g, w_branch_a,
           w_branch_b, w_out, w_router_group, b_router_group, w_router_expert, b_router_expert, w_gate_e, w_up_e,
           w_down_e, final_g):
    nb, seq, d = x_prompt.shape
    ns = x_sample.shape[0]
    assert w_mod.shape[0] == 1 and x_sample.shape[1] == 1
    n_p = nb * seq
    n = n_p + ns
    nheads = state_hgrn.shape[2]
    hgw = nheads * HG_DK
    n_ssm_heads = state_ssm.shape[2]
    inner = n_ssm_heads * SSM_HEADDIM
    ng = n_ssm_heads // SSM_HPG
    cdim = inner + 2 * ng * SSM_STATE
    col_q, col_f, col_v, col_og = 0, hgw, 2 * hgw, 2 * hgw + d
    col_z = 2 * hgw + 2 * d
    col_x = col_z + inner
    n_main = col_x + cdim
    col_gab = n_main + n_ssm_heads
    row = lambda a: a.reshape(1, -1)

    lb = jnp.cumsum(jax.nn.softmax(lb_params.astype(F32), axis=0), axis=0)[0]

    n_c = ns + nb
    n_c_pad = -(-n_c // 8) * 8
    c_all = jnp.concatenate([c_sample, c_prompt, jnp.zeros((n_c_pad - n_c, d), F32)], axis=0)
    mod = _mm(c_all, w_mod[0], tm=n_c_pad, tn=512, name="mod", a_fn=_silu,
              extras=[(row(b_mod[0]), pl.BlockSpec((1, 512), lambda i, j: (0, j)))],
              epilogue=lambda acc, b: acc + b)
    p_mod = dict(nb=nb, per_row=False, mod_row0=ns)
    s_mod = dict(nb=1, nt=1, tr=ns, per_row=True, mod_row0=0)

    xp2 = x_prompt.reshape(n_p, d)
    xs2 = x_sample.reshape(ns, d)
    h = _norm_mod(xp2, mod, row(norm1_g[0]), nt=seq // 256, tr=256, sc_blk=1, sh_blk=0, into=None, total_rows=n,
                  out_row0=0, name="norm1_prompt", **p_mod)
    h = _norm_mod(xs2, mod, row(norm1_g[0]), sc_blk=1, sh_blk=0, into=h, total_rows=n, out_row0=n_p,
                  name="norm1_sample", **s_mod)

    assert n_ssm_heads == LANES
    u = _mm(h, w_in[0], tm=MM_TM, tn=MM_TN_WIDE, ncols=n_main, name="in_proj")
    u_t = _mm(h, w_in[0], tm=MM_TM, tn=MM_TN_WIDE, col0=n_main, ncols=2 * d, name="in_proj_tail")
    u_l = _mm(h, w_in[0], tm=MM_TM, tn=LANES, col0=n_main + 2 * d, ncols=LANES, name="in_proj_last")
    u_dt = u_t[:, :LANES]

    gn_h = row(hg_norm_g[0])
    o_a, hg_p = _hgrn_prompt(u, row(lb), gn_h, nb=nb, seq=seq, nheads=nheads, col_q=col_q, col_f=col_f,
                             col_v=col_v, col_og=col_og, total_rows=n, tb=512)
    f_t = u[n_p:, col_f:col_f + hgw].reshape(ns // STEP_TOKENS, STEP_TOKENS, hgw).transpose(0, 2, 1)
    o_a, hg_s = _hgrn_step(u, f_t, lb.reshape(hgw, 1), gn_h, state_hgrn[0], row0=n_p, nheads=nheads,
                           col_q=col_q, col_v=col_v, col_og=col_og, into=o_a)

    gn_s = row(ssm_norm_g[0])
    y_b, ssm_p = _ssd_prompt(u, u_dt, conv_w[0], row(conv_b[0]), row(dt_bias[0]), row(a_log[0]), row(d_skip[0]),
                             gn_s, nb=nb, seq=seq, col_z=col_z, col_x=col_x, total_rows=n, tb=1024)
    xs, bm, cm, xdt, da = _ssd_prep(u, u_dt, state_conv[0].reshape(ns, (SSM_CONV - 1) * cdim), conv_w[0],
                                    row(conv_b[0]), row(dt_bias[0]), row(a_log[0]), row0=n_p, n=ns, col_x=col_x)
    cols = lambda a: a.reshape(ns // STEP_TOKENS, STEP_TOKENS, inner).transpose(0, 2, 1)
    dsk_full = jnp.repeat(d_skip[0], SSM_HEADDIM).reshape(1, inner)
    y_b, ssm_s = _ssd_step(u, xs, bm, cm, cols(xdt), cols(da), dsk_full, gn_s, state_ssm[0], row0=n_p,
                           col_z=col_z, into=y_b)
    xbc_rows = lambda r0, r1: lax.slice(u, (r0, col_x), (r1, col_x + cdim))
    conv_p = jnp.stack([xbc_rows((b + 1) * seq - (SSM_CONV - 1), (b + 1) * seq) for b in range(nb)])
    conv_s = jnp.concatenate([state_conv[0][:, 1:], xbc_rows(n_p, n)[:, None, :]], axis=1)

    merged = _branch_merge(o_a, y_b, w_branch_a[0], w_branch_b[0], u_t, u_l)
    proj = _mm(merged, w_out[0], tm=MM_TM, tn=MM_TN_WIDE, name="out_proj")

    wr = jnp.concatenate([w_router_group[0], w_router_expert[0],
                          jnp.zeros((d, LANES - N_GROUPS - N_EXPERTS), F32)], axis=1)
    wr_hi, wr_lo, _ = _split3(wr)
    wr3 = jnp.concatenate([wr_hi, wr_lo, wr_hi], axis=0)
    br = jnp.concatenate([b_router_group[0], b_router_expert[0],
                          jnp.zeros((LANES - N_GROUPS - N_EXPERTS,), F32)]).reshape(1, LANES)
    g2n = row(norm2_g[0])
    x1, h2, route = _resid_norm_router(xp2, proj, mod, g2n, wr3, br, nt=seq // 128, tr=128, gate_blk=2, sc_blk=4,
                                       sh_blk=3, proj_row0=0, intos=[None] * 3, total_rows=n,
                                       name="norm2_prompt", **p_mod)
    x1, h2, route = _resid_norm_router(xs2, proj, mod, g2n, wr3, br, gate_blk=2, sc_blk=4, sh_blk=3,
                                       proj_row0=n_p, intos=[x1, h2, route], total_rows=n,
                                       name="norm2_sample", **s_mod)

    e1 = route[:, 2].astype(jnp.int32)
    e2 = route[:, 3].astype(jnp.int32)
    n_tiles = -(-(2 * n + N_EXPERTS * (MOE_TILE - 1)) // MOE_TILE)
    pos, src, steps = _moe_plan(e1, e2, n_tiles)
    y_sorted = _moe_experts(h2, src, w_gate_e[0], w_up_e[0], w_down_e[0], steps, n_tiles)

    fg = row(final_g)
    y_prompt = _final(x1, y_sorted, pos, route, mod, fg, nt=seq // 128, tr=128, gate_blk=5, row0=0, out_rows=n_p,
                      name="final_prompt", **p_mod)
    y_sample = _final(x1, y_sorted, pos, route, mod, fg, gate_blk=5, row0=n_p, out_rows=ns, name="final_sample",
                      **s_mod)

    return (y_prompt.reshape(nb, seq, d), y_sample.reshape(ns, 1, d),
            hg_p[None], conv_p[None], ssm_p.reshape(1, nb, n_ssm_heads, SSM_HEADDIM, SSM_STATE),
            hg_s[None], conv_s[None], ssm_s[None])
```
